```python
import jax, jax.numpy as jnp
from jax import lax
import numpy as np

D_MODEL = 1024
BATCH = 16
SEQ = 2048
DEPTH = 4
DEC_BATCH = 128
DEC_SEQ = 1
PAST_LEN = 8192
PAGE_SIZE = 128

D_MIX = D_MODEL
CONV_DIM = D_MIX // 2
CONV_GROUPS = 8
CONV_WIDTH = 3
N_HEADS = 8
QK_NOPE = 64
QK_ROPE = 32
V_HEAD = 64
ATTN_DIM = N_HEADS * V_HEAD
Q_LORA = D_MODEL // 4
KV_LORA = D_MODEL // 8
D_FF = 2816
ROPE_THETA = 10000.0
RMS_EPS = 1e-6
Q_BLOCK = 128
IN_COLS = 3 * CONV_DIM + Q_LORA + KV_LORA + QK_ROPE
SPLITS = (CONV_DIM, 2 * CONV_DIM, 3 * CONV_DIM, 3 * CONV_DIM + Q_LORA,
          3 * CONV_DIM + Q_LORA + KV_LORA)

kernel_name = 'hymba_shortconv_mla_macaron_step'


def rmsnorm(x, g):
    xf = x.astype(jnp.float32)
    y = xf * lax.rsqrt(jnp.mean(xf * xf, axis=-1, keepdims=True) + RMS_EPS)
    return (y * g.astype(jnp.float32)).astype(x.dtype)


def rope(x, pos):
    half = x.shape[-1] // 2
    inv_freq = 1.0 / (ROPE_THETA ** (jnp.arange(half, dtype=jnp.float32) / half))
    ang = pos.astype(jnp.float32)[:, None] * inv_freq[None, :]
    cos = jnp.cos(ang)[:, None, :]
    sin = jnp.sin(ang)[:, None, :]
    xf = x.astype(jnp.float32)
    x1, x2 = xf[..., :half], xf[..., half:]
    return jnp.concatenate([x1 * cos - x2 * sin, x2 * cos + x1 * sin], axis=-1).astype(x.dtype)


def swiglu(h, wg, wu, wd):
    return (jax.nn.silu(h @ wg) * (h @ wu)) @ wd


def mixer_project(h, w_in, q_a_norm, w_q_b, kv_a_norm, pos):
    b, s, _ = h.shape
    proj = h @ w_in
    b_g, c_g, xc, q_c, kv_c, k_pe = jnp.split(proj, SPLITS, axis=-1)
    u = c_g * xc
    q = (rmsnorm(q_c, q_a_norm) @ w_q_b).reshape(b, s, N_HEADS, QK_NOPE + QK_ROPE)
    q_nope = q[..., :QK_NOPE]
    q_pe = rope(q[..., QK_NOPE:], pos)
    ckv = rmsnorm(kv_c, kv_a_norm)
    kpe = rope(k_pe[:, :, None, :], pos)[:, :, 0, :]
    return b_g, u, q_nope, q_pe, ckv, kpe


def short_conv(u_hist, conv_w, s):
    return sum(conv_w[k] * u_hist[:, k:k + s] for k in range(CONV_WIDTH))


def mixer_output(conv_y, attn_o, gn_conv, gn_attn, w_o):
    merged = jnp.concatenate([rmsnorm(conv_y, gn_conv), rmsnorm(attn_o, gn_attn)], axis=-1)
    return merged @ w_o


def prompt_mla(q_nope, q_pe, ckv, kpe, w_kv_b):
    b, s = ckv.shape[0], ckv.shape[1]
    kv = (ckv @ w_kv_b).reshape(b, s, N_HEADS, QK_NOPE + V_HEAD)
    k_nope, v = kv[..., :QK_NOPE], kv[..., QK_NOPE:]
    scale = (QK_NOPE + QK_ROPE) ** -0.5
    nb = s // Q_BLOCK
    qn_b = q_nope.reshape(b, nb, Q_BLOCK, N_HEADS, QK_NOPE).transpose(1, 0, 2, 3, 4)
    qp_b = q_pe.reshape(b, nb, Q_BLOCK, N_HEADS, QK_ROPE).transpose(1, 0, 2, 3, 4)
    key_pos = jnp.arange(s, dtype=jnp.int32)

    def block(args):
        qn, qp, i = args
        sc = (jnp.einsum('bqhd,bkhd->bhqk', qn, k_nope).astype(jnp.float32)
              + jnp.einsum('bqhr,bkr->bhqk', qp, kpe).astype(jnp.float32)) * scale
        q_pos = i * Q_BLOCK + jnp.arange(Q_BLOCK, dtype=jnp.int32)
        mask = key_pos[None, :] <= q_pos[:, None]
        sc = jnp.where(mask[None, None], sc, -jnp.inf)
        p = jax.nn.softmax(sc, axis=-1).astype(v.dtype)
        return jnp.einsum('bhqk,bkhv->bqhv', p, v)

    o = lax.map(block, (qn_b, qp_b, jnp.arange(nb, dtype=jnp.int32)))
    return o.transpose(1, 0, 2, 3, 4).reshape(b, s, ATTN_DIM)


def sample_mla(q_nope, q_pe, ckv_new, kpe_new, ckv_past, kpe_past, w_kv_b):
    b, s = ckv_new.shape[0], ckv_new.shape[1]
    past = ckv_past.shape[1]
    w = w_kv_b.reshape(KV_LORA, N_HEADS, QK_NOPE + V_HEAD)
    w_uk, w_uv = w[..., :QK_NOPE], w[..., QK_NOPE:]
    scale = (QK_NOPE + QK_ROPE) ** -0.5
    q_lat = jnp.einsum('bshn,chn->bshc', q_nope, w_uk)
    s_past = (jnp.einsum('bshc,btc->bhst', q_lat, ckv_past).astype(jnp.float32)
              + jnp.einsum('bshr,btr->bhst', q_pe, kpe_past).astype(jnp.float32))
    s_new = (jnp.einsum('bshc,btc->bhst', q_lat, ckv_new).astype(jnp.float32)
             + jnp.einsum('bshr,btr->bhst', q_pe, kpe_new).astype(jnp.float32))
    causal = jnp.tril(jnp.ones((s, s), dtype=bool))
    s_new = jnp.where(causal[None, None], s_new, -jnp.inf)
    p = jax.nn.softmax(jnp.concatenate([s_past, s_new], axis=-1) * scale, axis=-1).astype(ckv_new.dtype)
    o_lat = (jnp.einsum('bhst,btc->bshc', p[..., :past], ckv_past)
             + jnp.einsum('bhst,btc->bshc', p[..., past:], ckv_new))
    return jnp.einsum('bshc,chv->bshv', o_lat, w_uv).reshape(b, s, ATTN_DIM)


def setup_inputs(seed: int = 0) -> dict:
    key = jax.random.key(seed)
    ks = jax.random.split(key, 32)
    n_pages = PAST_LEN // PAGE_SIZE
    n_pool = (DEC_BATCH * n_pages * 5) // 4

    def dense(k, shape, fan_in):
        return jax.random.normal(k, shape, jnp.float32) * (fan_in ** -0.5)

    def gain(k, shape):
        return 1.0 + 0.01 * jax.random.normal(k, shape, jnp.float32)

    page_table = jax.random.permutation(ks[5], n_pool)[:DEC_BATCH * n_pages]
    page_table = page_table.reshape(DEC_BATCH, n_pages).astype(jnp.int32)
    return {
        'x_prompt': jax.random.normal(ks[0], (BATCH, SEQ, D_MODEL), jnp.float32),
        'x_sample': jax.random.normal(ks[1], (DEC_BATCH, DEC_SEQ, D_MODEL), jnp.float32),
        'state_conv': jax.random.normal(ks[2], (DEPTH, DEC_BATCH, CONV_WIDTH - 1, CONV_DIM), jnp.float32),
        'cache_ckv': jax.random.normal(ks[3], (DEPTH, n_pool, PAGE_SIZE, KV_LORA), jnp.float32),
        'cache_kpe': jax.random.normal(ks[4], (DEPTH, n_pool, PAGE_SIZE, QK_ROPE), jnp.float32),
        'page_table': page_table,
        'norm_ffn1': gain(ks[6], (DEPTH, D_MODEL)),
        'w_ffn1_gate': dense(ks[7], (DEPTH, D_MODEL, D_FF), D_MODEL),
        'w_ffn1_up': dense(ks[8], (DEPTH, D_MODEL, D_FF), D_MODEL),
        'w_ffn1_down': dense(ks[9], (DEPTH, D_FF, D_MODEL), D_FF),
        'norm_mix': gain(ks[10], (DEPTH, D_MODEL)),
        'w_in': dense(ks[11], (DEPTH, D_MODEL, IN_COLS), D_MODEL),
        'conv_w': dense(ks[12], (DEPTH, CONV_WIDTH, CONV_DIM), CONV_WIDTH),
        'q_a_norm': gain(ks[13], (DEPTH, Q_LORA)),
        'w_q_b': dense(ks[14], (DEPTH, Q_LORA, N_HEADS * (QK_NOPE + QK_ROPE)), Q_LORA),
        'kv_a_norm': gain(ks[15], (DEPTH, KV_LORA)),
        'w_kv_b': dense(ks[16], (DEPTH, KV_LORA, N_HEADS * (QK_NOPE + V_HEAD)), KV_LORA),
        'gn_conv': gain(ks[17], (DEPTH, CONV_DIM)),
        'gn_attn': gain(ks[18], (DEPTH, ATTN_DIM)),
        'w_o': dense(ks[19], (DEPTH, D_MIX, D_MODEL), D_MIX),
        'norm_ffn2': gain(ks[20], (DEPTH, D_MODEL)),
        'w_ffn2_gate': dense(ks[21], (DEPTH, D_MODEL, D_FF), D_MODEL),
        'w_ffn2_up': dense(ks[22], (DEPTH, D_MODEL, D_FF), D_MODEL),
        'w_ffn2_down': dense(ks[23], (DEPTH, D_FF, D_MODEL), D_FF),
        'final_norm': gain(ks[24], (D_MODEL,)),
    }


def reference(x_prompt, x_sample, state_conv, cache_ckv, cache_kpe, page_table,
              norm_ffn1, w_ffn1_gate, w_ffn1_up, w_ffn1_down,
              norm_mix, w_in, conv_w, q_a_norm, w_q_b, kv_a_norm, w_kv_b,
              gn_conv, gn_attn, w_o,
              norm_ffn2, w_ffn2_gate, w_ffn2_up, w_ffn2_down, final_norm):
    pos_p = jnp.arange(SEQ, dtype=jnp.int32)
    pos_s = PAST_LEN + jnp.arange(DEC_SEQ, dtype=jnp.int32)
    yp, ys = x_prompt, x_sample
    conv_p, ckv_p, kpe_p, conv_s, ckv_s, kpe_s = [], [], [], [], [], []
    for l in range(DEPTH):
        yp = yp + 0.5 * swiglu(rmsnorm(yp, norm_ffn1[l]), w_ffn1_gate[l], w_ffn1_up[l], w_ffn1_down[l])
        ys = ys + 0.5 * swiglu(rmsnorm(ys, norm_ffn1[l]), w_ffn1_gate[l], w_ffn1_up[l], w_ffn1_down[l])

        h = rmsnorm(yp, norm_mix[l])
        b_g, u, qn, qp, ckv, kpe = mixer_project(h, w_in[l], q_a_norm[l], w_q_b[l], kv_a_norm[l], pos_p)
        u_hist = jnp.concatenate([jnp.zeros((BATCH, CONV_WIDTH - 1, CONV_DIM), u.dtype), u], axis=1)
        conv_y = b_g * short_conv(u_hist, conv_w[l], SEQ)
        attn_o = prompt_mla(qn, qp, ckv, kpe, w_kv_b[l])
        yp = yp + mixer_output(conv_y, attn_o, gn_conv[l], gn_attn[l], w_o[l])
        conv_p.append(u_hist[:, -(CONV_WIDTH - 1):])
        ckv_p.append(ckv)
        kpe_p.append(kpe)

        h = rmsnorm(ys, norm_mix[l])
        b_g, u, qn, qp, ckv, kpe = mixer_project(h, w_in[l], q_a_norm[l], w_q_b[l], kv_a_norm[l], pos_s)
        u_hist = jnp.concatenate([state_conv[l], u], axis=1)
        conv_y = b_g * short_conv(u_hist, conv_w[l], DEC_SEQ)
        ckv_past = cache_ckv[l][page_table].reshape(DEC_BATCH, -1, KV_LORA)
        kpe_past = cache_kpe[l][page_table].reshape(DEC_BATCH, -1, QK_ROPE)
        attn_o = sample_mla(qn, qp, ckv, kpe, ckv_past, kpe_past, w_kv_b[l])
        ys = ys + mixer_output(conv_y, attn_o, gn_conv[l], gn_attn[l], w_o[l])
        conv_s.append(u_hist[:, -(CONV_WIDTH - 1):])
        ckv_s.append(ckv)
        kpe_s.append(kpe)

        yp = yp + 0.5 * swiglu(rmsnorm(yp, norm_ffn2[l]), w_ffn2_gate[l], w_ffn2_up[l], w_ffn2_down[l])
        ys = ys + 0.5 * swiglu(rmsnorm(ys, norm_ffn2[l]), w_ffn2_gate[l], w_ffn2_up[l], w_ffn2_down[l])

    y_prompt = rmsnorm(yp, final_norm)
    y_sample = rmsnorm(ys, final_norm)
    new_conv_prompt = jnp.stack(conv_p)
    new_ckv_prompt = jnp.stack(ckv_p)
    new_kpe_prompt = jnp.stack(kpe_p)
    new_conv_sample = jnp.stack(conv_s)
    new_ckv_sample = jnp.stack(ckv_s)
    new_kpe_sample = jnp.stack(kpe_s)
    return (y_prompt, y_sample, new_conv_prompt, new_ckv_prompt, new_kpe_prompt,
            new_conv_sample, new_ckv_sample, new_kpe_sample)
```

```python
import functools

import jax
import jax.numpy as jnp
from jax import lax
from jax.experimental import pallas as pl
from jax.experimental.pallas import tpu as pltpu

N_HEADS = 8
QK_NOPE = 64
QK_ROPE = 32
V_HEAD = 64
CONV_WIDTH = 3
ROPE_THETA = 10000.0
RMS_EPS = 1e-6

LANES = 128
SUBLANES = 8
MXU_COLS = 256
VMEM_LIMIT_BYTES = 56 * 2**20

ROW_TILE = 512
ATTN_TILE = 256
HALF_ROPE = QK_ROPE // 2
ROPE_LO = QK_NOPE

F32 = jnp.float32
BF16 = jnp.bfloat16


def _const_spec(shape):
    return pl.BlockSpec(shape, lambda *_: (0,) * len(shape), pipeline_mode=pl.Buffered(1))


def _params(*semantics):
    return pltpu.CompilerParams(dimension_semantics=semantics, vmem_limit_bytes=VMEM_LIMIT_BYTES)


def _rms(x, g):
    return x * lax.rsqrt(jnp.mean(x * x, axis=-1, keepdims=True) + RMS_EPS) * g


def _ff_chunks(d_ff):
    half = -(-(d_ff // 2) // MXU_COLS) * MXU_COLS
    return ((0, half), (half, d_ff)) if 0 < half < d_ff else ((0, d_ff),)


def _swiglu_residual(x, g_ref, wg_ref, wu_ref, wd_ref):
    xn = _rms(x, g_ref[...]).astype(BF16)
    out = None
    for f0, f1 in _ff_chunks(wg_ref.shape[1]):
        hg = jnp.dot(xn, wg_ref[:, f0:f1], preferred_element_type=F32)
        hu = jnp.dot(xn, wu_ref[:, f0:f1], preferred_element_type=F32)
        a = (hg * jax.nn.sigmoid(hg) * hu).astype(BF16)
        o = jnp.dot(a, wd_ref[f0:f1, :], preferred_element_type=F32)
        out = o if out is None else out + o
    return x + 0.5 * out


def _ffn_kernel(*refs, mix, decode, final):
    refs = list(refs)
    o_ref = refs.pop()
    x = refs.pop(0)[...]
    if mix:
        conv_ref, attn_ref, gn_attn_ref, wo_ref = refs[:4]
        refs = refs[4:]
        attn = attn_ref[...]
        if decode:
            wuv_ref = refs.pop(0)
            attn = jnp.dot(attn.astype(BF16), wuv_ref[...], preferred_element_type=F32)
        merged = jnp.concatenate(
            [conv_ref[...], _rms(attn, gn_attn_ref[...]).astype(BF16)], axis=-1)
        x = x + jnp.dot(merged, wo_ref[...], preferred_element_type=F32)
    g_ref, wg_ref, wu_ref, wd_ref = refs[:4]
    y = _swiglu_residual(x, g_ref, wg_ref, wu_ref, wd_ref)
    if final:
        y = _rms(y, refs[4][...])
    o_ref[...] = y


def _ffn_call(x, ffn_w, *, mix_in=None, wuv_bd=None, final_g=None, name):
    t, d = x.shape
    tm = min(ROW_TILE, t)
    assert t % tm == 0
    row = lambda w: pl.BlockSpec((tm, w), lambda i: (i, 0))
    args, specs = [x], [row(d)]
    if mix_in is not None:
        conv_n, attn, gn_attn, w_o = mix_in
        args += [conv_n, attn, gn_attn, w_o]
        specs += [row(conv_n.shape[1]), row(attn.shape[1]), _const_spec(gn_attn.shape),
                  _const_spec(w_o.shape)]
        if wuv_bd is not None:
            args.append(wuv_bd)
            specs.append(_const_spec(wuv_bd.shape))
    args += list(ffn_w)
    specs += [_const_spec(w.shape) for w in ffn_w]
    if final_g is not None:
        args.append(final_g)
        specs.append(_const_spec(final_g.shape))
    kern = functools.partial(_ffn_kernel, mix=mix_in is not None, decode=wuv_bd is not None,
                             final=final_g is not None)
    return pl.pallas_call(
        kern, grid=(t // tm,), in_specs=specs, out_specs=row(d),
        out_shape=jax.ShapeDtypeStruct((t, d), F32),
        compiler_params=_params("parallel"), name=name)(*args)


def _rope_slab(x, tab_ref):
    return (x * tab_ref[0]
            + pltpu.roll(x, LANES - HALF_ROPE, 1) * tab_ref[1]
            + pltpu.roll(x, HALF_ROPE, 1) * tab_ref[2])


def _proj_common(x_ref, gmix_ref, win_ref, gq_ref, wq_ref, gkv_ref, tab_ref, conv_dim, q_lora,
                 kv_lora):
    h = _rms(x_ref[...], gmix_ref[...]).astype(BF16)
    proj = jnp.dot(h, win_ref[...], preferred_element_type=F32)
    c0 = 3 * conv_dim
    b_g = proj[:, :conv_dim]
    u = proj[:, conv_dim:2 * conv_dim] * proj[:, 2 * conv_dim:c0]
    q_c = proj[:, c0:c0 + q_lora]
    kv_c = proj[:, c0 + q_lora:c0 + q_lora + kv_lora]
    kpe_slab = proj[:, c0 + q_lora + kv_lora:]
    q = jnp.dot(_rms(q_c, gq_ref[...]).astype(BF16), wq_ref[...], preferred_element_type=F32)
    q = jnp.concatenate(
        [_rope_slab(q[:, hd * LANES:(hd + 1) * LANES], tab_ref) for hd in range(N_HEADS)], axis=-1)
    ckv = _rms(kv_c, gkv_ref[...])
    kpe_slab = _rope_slab(kpe_slab, tab_ref)
    return b_g, u, q, ckv, kpe_slab


def _proj_prompt_kernel(x_ref, gmix_ref, win_ref, gq_ref, wq_ref, gkv_ref, wkv_ref, convw_ref,
                        gconv_ref, tab_ref,
                        convn_ref, q_ref, k_ref, v_ref, ckv_ref, kpe_ref, newconv_ref,
                        uext_ref, *, tiles_per_seq):
    tm = x_ref.shape[0]
    conv_dim = convw_ref.shape[1]
    b_g, u, q, ckv, kpe_slab = _proj_common(
        x_ref, gmix_ref, win_ref, gq_ref, wq_ref, gkv_ref, tab_ref, conv_dim,
        gq_ref.shape[1], gkv_ref.shape[1])

    @pl.when(pl.program_id(0) % tiles_per_seq == 0)
    def _():
        uext_ref[0:SUBLANES, :] = jnp.zeros((SUBLANES, conv_dim), F32)

    uext_ref[SUBLANES:SUBLANES + tm, :] = u
    u1 = uext_ref[SUBLANES - 1:SUBLANES - 1 + tm, :]
    u2 = uext_ref[SUBLANES - 2:SUBLANES - 2 + tm, :]
    conv_y = b_g * (convw_ref[0:1, :] * u2 + convw_ref[1:2, :] * u1 + convw_ref[2:3, :] * u)
    uext_ref[0:SUBLANES, :] = u[tm - SUBLANES:, :]
    newconv_ref[0] = u[tm - (CONV_WIDTH - 1):, :]
    convn_ref[...] = _rms(conv_y, gconv_ref[...]).astype(BF16)

    q_ref[...] = q.astype(BF16)
    ckv_ref[...] = ckv
    kpe_ref[...] = kpe_slab[:, ROPE_LO:ROPE_LO + QK_ROPE]
    kv = jnp.dot(ckv.astype(BF16), wkv_ref[...], preferred_element_type=F32)
    k_w = N_HEADS * LANES
    k_ref[...] = (kv[:, :k_w] + jnp.concatenate([kpe_slab] * N_HEADS, axis=-1)).astype(BF16)
    v_ref[...] = kv[:, k_w:].astype(BF16)


def _proj_decode_kernel(x_ref, gmix_ref, win_ref, gq_ref, wq_ref, gkv_ref, wuk_ref, convw_ref,
                        gconv_ref, tab_ref, s0_ref, s1_ref,
                        convn_ref, qlat_ref, q_ref, ckv_ref, kpe_ref, kpeslab_ref, u_ref):
    conv_dim = convw_ref.shape[1]
    b_g, u, q, ckv, kpe_slab = _proj_common(
        x_ref, gmix_ref, win_ref, gq_ref, wq_ref, gkv_ref, tab_ref, conv_dim,
        gq_ref.shape[1], gkv_ref.shape[1])
    conv_y = b_g * (convw_ref[0:1, :] * s0_ref[...] + convw_ref[1:2, :] * s1_ref[...]
                    + convw_ref[2:3, :] * u)
    convn_ref[...] = _rms(conv_y, gconv_ref[...]).astype(BF16)
    u_ref[...] = u
    qb = q.astype(BF16)
    q_ref[...] = qb
    qlat_ref[...] = jnp.dot(qb, wuk_ref[...], preferred_element_type=F32).astype(BF16)
    ckv_ref[...] = ckv
    kpe_ref[...] = kpe_slab[:, ROPE_LO:ROPE_LO + QK_ROPE]
    kpeslab_ref[...] = kpe_slab


def _proj_prompt_call(x, w, tab, batch, seq, name):
    t, d = x.shape
    tm = min(ROW_TILE, seq)
    assert seq % tm == 0 and tm >= SUBLANES
    tiles_per_seq = seq // tm
    conv_dim = w["conv_w"].shape[1]
    kv_lora = w["kv_a_norm"].shape[1]
    k_w, v_w = N_HEADS * LANES, N_HEADS * V_HEAD
    row = lambda width: pl.BlockSpec((tm, width), lambda i: (i, 0))
    consts = [w["norm_mix"], w["w_in"], w["q_a_norm"], w["w_q"], w["kv_a_norm"], w["w_kv"],
              w["conv_w"], w["gn_conv"]]
    in_specs = ([row(d)] + [_const_spec(c.shape) for c in consts]
                + [pl.BlockSpec((3, tm, LANES), lambda i: (0, i % tiles_per_seq, 0))])
    out_shape = (jax.ShapeDtypeStruct((t, conv_dim), BF16),
                 jax.ShapeDtypeStruct((t, k_w), BF16),
                 jax.ShapeDtypeStruct((t, k_w), BF16),
                 jax.ShapeDtypeStruct((t, v_w), BF16),
                 jax.ShapeDtypeStruct((t, kv_lora), F32),
                 jax.ShapeDtypeStruct((t, QK_ROPE), F32),
                 jax.ShapeDtypeStruct((batch, CONV_WIDTH - 1, conv_dim), F32))
    out_specs = (row(conv_dim), row(k_w), row(k_w), row(v_w), row(kv_lora), row(QK_ROPE),
                 pl.BlockSpec((1, CONV_WIDTH - 1, conv_dim), lambda i: (i // tiles_per_seq, 0, 0)))
    return pl.pallas_call(
        functools.partial(_proj_prompt_kernel, tiles_per_seq=tiles_per_seq),
        grid=(t // tm,), in_specs=in_specs, out_specs=out_specs, out_shape=out_shape,
        scratch_shapes=[pltpu.VMEM((tm + SUBLANES, conv_dim), F32)],
        compiler_params=_params("arbitrary"), name=name)(x, *consts, tab)


def _proj_decode_call(x, w, tab, s0, s1, name):
    t, d = x.shape
    conv_dim = w["conv_w"].shape[1]
    kv_lora = w["kv_a_norm"].shape[1]
    k_w = N_HEADS * LANES
    args = [x, w["norm_mix"], w["w_in"], w["q_a_norm"], w["w_q"], w["kv_a_norm"], w["w_uk_bd"],
            w["conv_w"], w["gn_conv"], tab, s0, s1]
    full = lambda a: pl.BlockSpec(a.shape, lambda i: (0,) * a.ndim)
    out_shape = (jax.ShapeDtypeStruct((t, conv_dim), BF16),
                 jax.ShapeDtypeStruct((t, k_w), BF16),
                 jax.ShapeDtypeStruct((t, k_w), BF16),
                 jax.ShapeDtypeStruct((t, kv_lora), F32),
                 jax.ShapeDtypeStruct((t, QK_ROPE), F32),
                 jax.ShapeDtypeStruct((t, LANES), F32),
                 jax.ShapeDtypeStruct((t, conv_dim), F32))
    return pl.pallas_call(
        _proj_decode_kernel, grid=(1,), in_specs=[full(a) for a in args],
        out_specs=tuple(pl.BlockSpec(s.shape, lambda i: (0, 0)) for s in out_shape),
        out_shape=out_shape, compiler_params=_params("arbitrary"), name=name)(*args)


def _attn_kernel(q_ref, k_ref, v_ref, o_ref, *, scale):
    tq = q_ref.shape[1]
    i = pl.program_id(2)
    row = lax.broadcasted_iota(jnp.int32, (tq, tq), 0)
    col = lax.broadcasted_iota(jnp.int32, (tq, tq), 1)
    outs = []
    for hh in range(2):
        q = q_ref[0, :, hh * LANES:(hh + 1) * LANES]

        def block(j, carry, masked, hh=hh, q=q):
            m, l, acc = carry
            start = pl.multiple_of(j * tq, tq)
            k = k_ref[0, pl.ds(start, tq), hh * LANES:(hh + 1) * LANES]
            v = v_ref[0, pl.ds(start, tq), :]
            s = lax.dot_general(q, k, (((1,), (1,)), ((), ())),
                                preferred_element_type=F32) * scale
            if masked:
                s = jnp.where(col <= row, s, -jnp.inf)
            m_new = jnp.maximum(m, jnp.max(s, axis=-1, keepdims=True))
            alpha = jnp.exp(m - m_new)
            p = jnp.exp(s - m_new)
            l = alpha * l + jnp.sum(p, axis=-1, keepdims=True)
            acc = alpha * acc + jnp.dot(p.astype(BF16), v, preferred_element_type=F32)
            return m_new, l, acc

        init = (jnp.full((tq, 1), -jnp.inf, F32), jnp.zeros((tq, 1), F32),
                jnp.zeros((tq, LANES), F32))
        carry = lax.fori_loop(0, i, functools.partial(block, masked=False), init)
        _, l, acc = block(i, carry, True)
        outs.append(acc / l)
    lane = lax.broadcasted_iota(jnp.int32, (tq, LANES), 1)
    o_ref[0] = jnp.where(lane < V_HEAD, outs[0], outs[1])


def _attn_call(q, k, v, name):
    b, s, _ = q.shape
    tq = min(ATTN_TILE, s)
    assert s % tq == 0 and 2 * V_HEAD == LANES
    scale = float((QK_NOPE + QK_ROPE) ** -0.5)
    return pl.pallas_call(
        functools.partial(_attn_kernel, scale=scale),
        grid=(b, N_HEADS // 2, s // tq),
        in_specs=[pl.BlockSpec((1, tq, 2 * LANES), lambda bi, p, i: (bi, i, p)),
                  pl.BlockSpec((1, s, 2 * LANES), lambda bi, p, i: (bi, 0, p)),
                  pl.BlockSpec((1, s, LANES), lambda bi, p, i: (bi, 0, p))],
        out_specs=pl.BlockSpec((1, tq, LANES), lambda bi, p, i: (bi, i, p)),
        out_shape=jax.ShapeDtypeStruct((b, s, N_HEADS * V_HEAD), F32),
        compiler_params=_params("parallel", "parallel", "arbitrary"), name=name)(q, k, v)


def _decode_attn_kernel(pt_ref, qlat_ref, q_ref, ckvn_ref, kpen_ref, cckv_hbm, ckpe_hbm,
                        o_ref, ckv_buf, kpe_buf, sem, *, layer, scale):
    b = pl.program_id(0)
    nb = pl.num_programs(0)
    n_pages = ckv_buf.shape[1]
    slot = b % 2

    def copies(bb, sl, p):
        page = pt_ref[bb, p]
        return (pltpu.make_async_copy(cckv_hbm.at[layer, page], ckv_buf.at[sl, p], sem.at[0, sl]),
                pltpu.make_async_copy(ckpe_hbm.at[layer, page], kpe_buf.at[sl, p], sem.at[1, sl]))

    def start_fetch(bb, sl):
        def body(p, c):
            for cp in copies(bb, sl, p):
                cp.start()
            return c
        lax.fori_loop(0, n_pages, body, 0)

    @pl.when(b == 0)
    def _():
        start_fetch(0, 0)

    @pl.when(b + 1 < nb)
    def _():
        start_fetch(b + 1, 1 - slot)

    def wait_body(p, c):
        for cp in copies(b, slot, p):
            cp.wait()
        return c
    lax.fori_loop(0, n_pages, wait_body, 0)

    q_lat = qlat_ref[0]
    q_slab = q_ref[0]
    q_pe = q_slab[:, ROPE_LO:ROPE_LO + QK_ROPE]
    ckv = ckv_buf[slot]
    ckv = ckv.reshape(ckv.shape[0] * ckv.shape[1], ckv.shape[2]).astype(BF16)
    kpe = kpe_buf[slot]
    kpe = kpe.reshape(kpe.shape[0] * kpe.shape[1], kpe.shape[2]).astype(BF16)
    nt = (((1,), (1,)), ((), ()))
    s = (lax.dot_general(q_lat, ckv, nt, preferred_element_type=F32)
         + lax.dot_general(q_pe, kpe, nt, preferred_element_type=F32)) * scale
    ckv_new = ckvn_ref[0]
    s_new = (jnp.sum(q_lat.astype(F32) * ckv_new, axis=-1, keepdims=True)
             + jnp.sum(q_slab.astype(F32) * kpen_ref[0], axis=-1, keepdims=True)) * scale
    m = jnp.maximum(jnp.max(s, axis=-1, keepdims=True), s_new)
    p = jnp.exp(s - m)
    p_new = jnp.exp(s_new - m)
    l = jnp.sum(p, axis=-1, keepdims=True) + p_new
    o = jnp.dot(p.astype(BF16), ckv, preferred_element_type=F32) + p_new * ckv_new
    o_ref[0] = o / l


def _decode_attn_call(page_table, qlat, q, ckv_new, kpe_new_slab, cache_ckv, cache_kpe, layer,
                      name):
    nb, n_pages = page_table.shape
    page, kv_lora = cache_ckv.shape[2], cache_ckv.shape[3]
    assert kv_lora == LANES
    heads3 = lambda a: a.reshape(nb, N_HEADS, LANES)
    one3 = lambda a: a.reshape(nb, 1, LANES)
    per_b = lambda rows: pl.BlockSpec((1, rows, LANES), lambda b, pt: (b, 0, 0))
    grid_spec = pltpu.PrefetchScalarGridSpec(
        num_scalar_prefetch=1, grid=(nb,),
        in_specs=[per_b(N_HEADS), per_b(N_HEADS), per_b(1), per_b(1),
                  pl.BlockSpec(memory_space=pl.ANY), pl.BlockSpec(memory_space=pl.ANY)],
        out_specs=per_b(N_HEADS),
        scratch_shapes=[pltpu.VMEM((2, n_pages, page, kv_lora), F32),
                        pltpu.VMEM((2, n_pages, page, cache_kpe.shape[3]), F32),
                        pltpu.SemaphoreType.DMA((2, 2))])
    scale = float((QK_NOPE + QK_ROPE) ** -0.5)
    out = pl.pallas_call(
        functools.partial(_decode_attn_kernel, layer=layer, scale=scale),
        grid_spec=grid_spec,
        out_shape=jax.ShapeDtypeStruct((nb, N_HEADS, LANES), F32),
        compiler_params=_params("arbitrary"), name=name)(
            page_table, heads3(qlat), heads3(q), one3(ckv_new), one3(kpe_new_slab),
            cache_ckv, cache_kpe)
    return out.reshape(nb, N_HEADS * LANES)


def _rope_tables(pos):
    inv_freq = 1.0 / (ROPE_THETA ** (jnp.arange(HALF_ROPE, dtype=F32) / HALF_ROPE))
    ang = pos.astype(F32)[:, None] * inv_freq[None, :]
    cos, sin = jnp.cos(ang), jnp.sin(ang)
    n = pos.shape[0]
    one = jnp.ones((n, ROPE_LO), F32)
    zero = lambda w: jnp.zeros((n, w), F32)
    tail = LANES - ROPE_LO - QK_ROPE
    c = jnp.concatenate([one, cos, cos, zero(tail)], axis=-1)
    s1 = jnp.concatenate([zero(ROPE_LO), -sin, zero(HALF_ROPE + tail)], axis=-1)
    s2 = jnp.concatenate([zero(ROPE_LO + HALF_ROPE), sin, zero(tail)], axis=-1)
    return jnp.stack([c, s1, s2])


def _layer_weights(l, norm_mix, w_in, conv_w, q_a_norm, w_q_b, kv_a_norm, w_kv_b, gn_conv,
                   gn_attn, w_o):
    d = w_in.shape[1]
    conv_dim, q_lora, kv_lora = conv_w.shape[2], q_a_norm.shape[1], kv_a_norm.shape[1]
    c1 = 3 * conv_dim + q_lora + kv_lora
    tail = LANES - ROPE_LO - QK_ROPE
    win = w_in[l]
    win_ext = jnp.concatenate(
        [win[:, :c1], jnp.zeros((d, ROPE_LO), F32), win[:, c1:], jnp.zeros((d, tail), F32)],
        axis=-1).astype(BF16)
    wq = w_q_b[l].reshape(q_lora, N_HEADS, QK_NOPE + QK_ROPE)
    wq_slab = jnp.concatenate([wq, jnp.zeros((q_lora, N_HEADS, tail), F32)], axis=-1)
    wq_slab = wq_slab.reshape(q_lora, N_HEADS * LANES).astype(BF16)
    wkv = w_kv_b[l].reshape(kv_lora, N_HEADS, QK_NOPE + V_HEAD)
    w_uk, w_uv = wkv[..., :QK_NOPE], wkv[..., QK_NOPE:]
    wk_slab = jnp.concatenate([w_uk, jnp.zeros((kv_lora, N_HEADS, LANES - QK_NOPE), F32)], axis=-1)
    wkv_ext = jnp.concatenate([wk_slab.reshape(kv_lora, N_HEADS * LANES),
                               w_uv.reshape(kv_lora, N_HEADS * V_HEAD)], axis=-1).astype(BF16)
    eye = jnp.eye(N_HEADS, dtype=F32)
    uk_rows = jnp.concatenate(
        [jnp.transpose(w_uk, (1, 2, 0)), jnp.zeros((N_HEADS, LANES - QK_NOPE, kv_lora), F32)],
        axis=1)
    w_uk_bd = (uk_rows[:, :, None, :] * eye[:, None, :, None]).reshape(
        N_HEADS * LANES, N_HEADS * kv_lora).astype(BF16)
    w_uv_bd = (jnp.transpose(w_uv, (1, 0, 2))[:, :, None, :] * eye[:, None, :, None]).reshape(
        N_HEADS * kv_lora, N_HEADS * V_HEAD).astype(BF16)
    return dict(norm_mix=norm_mix[l][None], w_in=win_ext, q_a_norm=q_a_norm[l][None], w_q=wq_slab,
                kv_a_norm=kv_a_norm[l][None], w_kv=wkv_ext, w_uk_bd=w_uk_bd, w_uv_bd=w_uv_bd,
                conv_w=conv_w[l], gn_conv=gn_conv[l][None], gn_attn=gn_attn[l][None],
                w_o=w_o[l].astype(BF16))


def kernel(x_prompt, x_sample, state_conv, cache_ckv, cache_kpe, page_table, norm_ffn1, w_ffn1_gate, w_ffn1_up, w_ffn1_down, norm_mix, w_in, conv_w, q_a_norm, w_q_b, kv_a_norm, w_kv_b, gn_conv, gn_attn, w_o, norm_ffn2, w_ffn2_gate, w_ffn2_up, w_ffn2_down, final_norm):
    batch, seq, d = x_prompt.shape
    dec_batch, dec_seq, _ = x_sample.shape
    assert dec_seq == 1
    depth = norm_mix.shape[0]
    past_len = page_table.shape[1] * cache_ckv.shape[2]
    conv_dim, kv_lora = conv_w.shape[2], kv_a_norm.shape[1]

    tab_p = _rope_tables(jnp.arange(seq, dtype=jnp.int32))
    tab_s = jnp.broadcast_to(_rope_tables(past_len + jnp.arange(dec_seq, dtype=jnp.int32)),
                             (3, dec_batch, LANES))
    yp = x_prompt.reshape(batch * seq, d)
    ys = x_sample.reshape(dec_batch, d)
    fin = final_norm[None]
    conv_p, ckv_p, kpe_p, conv_s, ckv_s, kpe_s = [], [], [], [], [], []
    for l in range(depth):
        w = _layer_weights(l, norm_mix, w_in, conv_w, q_a_norm, w_q_b, kv_a_norm, w_kv_b, gn_conv,
                           gn_attn, w_o)
        ffn1 = (norm_ffn1[l][None], w_ffn1_gate[l].astype(BF16), w_ffn1_up[l].astype(BF16),
                w_ffn1_down[l].astype(BF16))
        ffn2 = (norm_ffn2[l][None], w_ffn2_gate[l].astype(BF16), w_ffn2_up[l].astype(BF16),
                w_ffn2_down[l].astype(BF16))
        last = fin if l == depth - 1 else None

        yp = _ffn_call(yp, ffn1, name=f"ffn1_p{l}")
        conv_n, q, k, v, ckv, kpe, new_conv = _proj_prompt_call(yp, w, tab_p, batch, seq,
                                                                name=f"proj_p{l}")
        attn = _attn_call(q.reshape(batch, seq, -1), k.reshape(batch, seq, -1),
                          v.reshape(batch, seq, -1), name=f"attn_p{l}")
        yp = _ffn_call(yp, ffn2, mix_in=(conv_n, attn.reshape(batch * seq, -1), w["gn_attn"], w["w_o"]),
                       final_g=last, name=f"mix_ffn2_p{l}")
        conv_p.append(new_conv)
        ckv_p.append(ckv.reshape(batch, seq, kv_lora))
        kpe_p.append(kpe.reshape(batch, seq, QK_ROPE))

        ys = _ffn_call(ys, ffn1, name=f"ffn1_s{l}")
        s0, s1 = state_conv[l][:, 0, :], state_conv[l][:, 1, :]
        conv_n, qlat, q, ckv, kpe, kpe_slab, u = _proj_decode_call(ys, w, tab_s, s0, s1,
                                                                   name=f"proj_s{l}")
        o_lat = _decode_attn_call(page_table, qlat, q, ckv, kpe_slab, cache_ckv, cache_kpe, l,
                                  name=f"attn_s{l}")
        ys = _ffn_call(ys, ffn2, mix_in=(conv_n, o_lat, w["gn_attn"], w["w_o"]),
                       wuv_bd=w["w_uv_bd"], final_g=last, name=f"mix_ffn2_s{l}")
        conv_s.append(jnp.stack([s1, u], axis=1))
        ckv_s.append(ckv.reshape(dec_batch, dec_seq, kv_lora))
        kpe_s.append(kpe.reshape(dec_batch, dec_seq, QK_ROPE))

    return (yp.reshape(batch, seq, d), ys.reshape(dec_batch, dec_seq, d),
            jnp.stack(conv_p), jnp.stack(ckv_p), jnp.stack(kpe_p),
            jnp.stack(conv_s), jnp.stack(ckv_s), jnp.stack(kpe_s))
```

```python
import functools

import jax
import jax.numpy as jnp
from jax import lax
from jax.experimental import pallas as pl
from jax.experimental.pallas import tpu as pltpu

N_HEADS = 8
QK_NOPE = 64
QK_ROPE = 32
V_HEAD = 64
CONV_WIDTH = 3
ROPE_THETA = 10000.0
RMS_EPS = 1e-6

LANES = 128
SUBLANES = 8
MXU_COLS = 256
VMEM_LIMIT_BYTES = 56 * 2**20

ROW_TILE = 512
ATTN_TILE = 256
ATTN_HEAD_GROUP = 4
DMA_ISSUE_UNROLL = 8
DECODE_ROWS = 2
HALF_ROPE = QK_ROPE // 2
ROPE_LO = QK_NOPE
LOG2_E = 1.4426950408889634

F32 = jnp.float32
BF16 = jnp.bfloat16


def _const_spec(shape):
    return pl.BlockSpec(shape, lambda *_: (0,) * len(shape), pipeline_mode=pl.Buffered(1))


def _params(*semantics):
    return pltpu.CompilerParams(dimension_semantics=semantics, vmem_limit_bytes=VMEM_LIMIT_BYTES)


def _rms(x, g):
    return x * lax.rsqrt(jnp.mean(x * x, axis=-1, keepdims=True) + RMS_EPS) * g


def _ff_chunks(d_ff):
    half = -(-(d_ff // 2) // MXU_COLS) * MXU_COLS
    return ((0, half), (half, d_ff)) if 0 < half < d_ff else ((0, d_ff),)


def _swiglu_residual(x, g_ref, wg_ref, wu_ref, wd_ref):
    xn = _rms(x, g_ref[...]).astype(BF16)
    out = None
    for f0, f1 in _ff_chunks(wg_ref.shape[1]):
        hg = jnp.dot(xn, wg_ref[:, f0:f1], preferred_element_type=F32)
        hu = jnp.dot(xn, wu_ref[:, f0:f1], preferred_element_type=F32)
        a = (hg * jax.nn.sigmoid(hg) * hu).astype(BF16)
        o = jnp.dot(a, wd_ref[f0:f1, :], preferred_element_type=F32)
        out = o if out is None else out + o
    return x + 0.5 * out


def _ffn_kernel(*refs, mix, decode, final):
    refs = list(refs)
    o_ref = refs.pop()
    x = refs.pop(0)[...]
    if mix:
        conv_ref, attn_ref, wo_ref = refs[:3]
        refs = refs[3:]
        attn = attn_ref[...]
        if decode:
            wuv_ref, gn_attn_ref = refs[:2]
            refs = refs[2:]
            attn = jnp.dot(attn.astype(BF16), wuv_ref[...], preferred_element_type=F32)
            attn = _rms(attn, gn_attn_ref[...]).astype(BF16)
        merged = jnp.concatenate([conv_ref[...], attn], axis=-1)
        x = x + jnp.dot(merged, wo_ref[...], preferred_element_type=F32)
    g_ref, wg_ref, wu_ref, wd_ref = refs[:4]
    y = _swiglu_residual(x, g_ref, wg_ref, wu_ref, wd_ref)
    if final:
        y = _rms(y, refs[4][...])
    o_ref[...] = y


def _ffn_call(x, ffn_w, *, mix_in=None, decode_in=None, final_g=None, name):
    t, d = x.shape
    tm = min(ROW_TILE, t)
    assert t % tm == 0
    row = lambda w: pl.BlockSpec((tm, w), lambda i: (i, 0))
    args, specs = [x], [row(d)]
    if mix_in is not None:
        conv_n, attn, w_o = mix_in
        args += [conv_n, attn, w_o]
        specs += [row(conv_n.shape[1]), row(attn.shape[1]), _const_spec(w_o.shape)]
        if decode_in is not None:
            args += list(decode_in)
            specs += [_const_spec(a.shape) for a in decode_in]
    args += list(ffn_w)
    specs += [_const_spec(w.shape) for w in ffn_w]
    if final_g is not None:
        args.append(final_g)
        specs.append(_const_spec(final_g.shape))
    kern = functools.partial(_ffn_kernel, mix=mix_in is not None, decode=decode_in is not None,
                             final=final_g is not None)
    return pl.pallas_call(
        kern, grid=(t // tm,), in_specs=specs, out_specs=row(d),
        out_shape=jax.ShapeDtypeStruct((t, d), F32),
        compiler_params=_params("parallel"), name=name)(*args)


def _rope_slab(x, tab_ref):
    return (x * tab_ref[0]
            + pltpu.roll(x, LANES - HALF_ROPE, 1) * tab_ref[1]
            + pltpu.roll(x, HALF_ROPE, 1) * tab_ref[2])


def _proj_common(x_ref, gmix_ref, win_ref, gq_ref, wq_ref, gkv_ref, tab_ref, conv_dim, q_lora,
                 kv_lora):
    h = _rms(x_ref[...], gmix_ref[...]).astype(BF16)
    proj = jnp.dot(h, win_ref[...], preferred_element_type=F32)
    c0 = 3 * conv_dim
    b_g = proj[:, :conv_dim]
    u = proj[:, conv_dim:2 * conv_dim] * proj[:, 2 * conv_dim:c0]
    q_c = proj[:, c0:c0 + q_lora]
    kv_c = proj[:, c0 + q_lora:c0 + q_lora + kv_lora]
    kpe_slab = proj[:, c0 + q_lora + kv_lora:]
    q = jnp.dot(_rms(q_c, gq_ref[...]).astype(BF16), wq_ref[...], preferred_element_type=F32)
    q = jnp.concatenate(
        [_rope_slab(q[:, hd * LANES:(hd + 1) * LANES], tab_ref) for hd in range(N_HEADS)], axis=-1)
    ckv = _rms(kv_c, gkv_ref[...])
    kpe_slab = _rope_slab(kpe_slab, tab_ref)
    return b_g, u, q, ckv, kpe_slab


def _proj_prompt_kernel(x_ref, gmix_ref, win_ref, gq_ref, wq_ref, gkv_ref, wkv_ref, vone_ref,
                        convw_ref, gconv_ref, tab_ref,
                        convn_ref, q_ref, k_ref, v_ref, ckv_ref, kpe_ref, newconv_ref,
                        uext_ref, *, tiles_per_seq):
    tm = x_ref.shape[0]
    conv_dim = convw_ref.shape[1]
    b_g, u, q, ckv, kpe_slab = _proj_common(
        x_ref, gmix_ref, win_ref, gq_ref, wq_ref, gkv_ref, tab_ref, conv_dim,
        gq_ref.shape[1], gkv_ref.shape[1])

    @pl.when(pl.program_id(0) % tiles_per_seq == 0)
    def _():
        uext_ref[0:SUBLANES, :] = jnp.zeros((SUBLANES, conv_dim), F32)

    uext_ref[SUBLANES:SUBLANES + tm, :] = u
    u1 = uext_ref[SUBLANES - 1:SUBLANES - 1 + tm, :]
    u2 = uext_ref[SUBLANES - 2:SUBLANES - 2 + tm, :]
    conv_y = b_g * (convw_ref[0:1, :] * u2 + convw_ref[1:2, :] * u1 + convw_ref[2:3, :] * u)
    uext_ref[0:SUBLANES, :] = u[tm - SUBLANES:, :]
    newconv_ref[0] = u[tm - (CONV_WIDTH - 1):, :]
    convn_ref[...] = _rms(conv_y, gconv_ref[...]).astype(BF16)

    q_ref[...] = q.astype(BF16)
    ckv_ref[...] = ckv
    kpe_ref[...] = kpe_slab[:, ROPE_LO:ROPE_LO + QK_ROPE]
    kv = jnp.dot(ckv.astype(BF16), wkv_ref[...], preferred_element_type=F32)
    k_w = N_HEADS * LANES
    k_ref[...] = (kv[:, :k_w] + jnp.concatenate([kpe_slab] * N_HEADS, axis=-1)).astype(BF16)
    v_ref[...] = (kv[:, k_w:] + vone_ref[...]).astype(BF16)


def _proj_decode_kernel(x_ref, gmix_ref, win_ref, gq_ref, wq_ref, gkv_ref, wuk_ref, convw_ref,
                        gconv_ref, tab_ref, s0_ref, s1_ref,
                        convn_ref, qlat_ref, q_ref, ckv_ref, kpe_ref, kpeslab_ref, u_ref):
    conv_dim = convw_ref.shape[1]
    b_g, u, q, ckv, kpe_slab = _proj_common(
        x_ref, gmix_ref, win_ref, gq_ref, wq_ref, gkv_ref, tab_ref, conv_dim,
        gq_ref.shape[1], gkv_ref.shape[1])
    conv_y = b_g * (convw_ref[0:1, :] * s0_ref[...] + convw_ref[1:2, :] * s1_ref[...]
                    + convw_ref[2:3, :] * u)
    convn_ref[...] = _rms(conv_y, gconv_ref[...]).astype(BF16)
    u_ref[...] = u
    qb = q.astype(BF16)
    q_ref[...] = qb
    qlat_ref[...] = jnp.dot(qb, wuk_ref[...], preferred_element_type=F32).astype(BF16)
    ckv_ref[...] = ckv
    kpe_ref[...] = kpe_slab[:, ROPE_LO:ROPE_LO + QK_ROPE]
    kpeslab_ref[...] = kpe_slab


def _proj_prompt_call(x, w, tab, batch, seq, name):
    t, d = x.shape
    tm = min(ROW_TILE, seq)
    assert seq % tm == 0 and tm >= SUBLANES
    tiles_per_seq = seq // tm
    conv_dim = w["conv_w"].shape[1]
    kv_lora = w["kv_a_norm"].shape[1]
    k_w = N_HEADS * LANES
    row = lambda width: pl.BlockSpec((tm, width), lambda i: (i, 0))
    consts = [w["norm_mix"], w["w_in"], w["q_a_norm"], w["w_q"], w["kv_a_norm"], w["w_kv"],
              w["v_one"], w["conv_w"], w["gn_conv"]]
    in_specs = ([row(d)] + [_const_spec(c.shape) for c in consts]
                + [pl.BlockSpec((3, tm, LANES), lambda i: (0, i % tiles_per_seq, 0))])
    out_shape = (jax.ShapeDtypeStruct((t, conv_dim), BF16),
                 jax.ShapeDtypeStruct((t, k_w), BF16),
                 jax.ShapeDtypeStruct((t, k_w), BF16),
                 jax.ShapeDtypeStruct((t, k_w), BF16),
                 jax.ShapeDtypeStruct((t, kv_lora), F32),
                 jax.ShapeDtypeStruct((t, QK_ROPE), F32),
                 jax.ShapeDtypeStruct((batch, CONV_WIDTH - 1, conv_dim), F32))
    out_specs = (row(conv_dim), row(k_w), row(k_w), row(k_w), row(kv_lora), row(QK_ROPE),
                 pl.BlockSpec((1, CONV_WIDTH - 1, conv_dim), lambda i: (i // tiles_per_seq, 0, 0)))
    return pl.pallas_call(
        functools.partial(_proj_prompt_kernel, tiles_per_seq=tiles_per_seq),
        grid=(t // tm,), in_specs=in_specs, out_specs=out_specs, out_shape=out_shape,
        scratch_shapes=[pltpu.VMEM((tm + SUBLANES, conv_dim), F32)],
        compiler_params=_params("arbitrary"), name=name)(x, *consts, tab)


def _proj_decode_call(x, w, tab, s0, s1, name):
    t, d = x.shape
    conv_dim = w["conv_w"].shape[1]
    kv_lora = w["kv_a_norm"].shape[1]
    k_w = N_HEADS * LANES
    args = [x, w["norm_mix"], w["w_in"], w["q_a_norm"], w["w_q"], w["kv_a_norm"], w["w_uk_bd"],
            w["conv_w"], w["gn_conv"], tab, s0, s1]
    full = lambda a: pl.BlockSpec(a.shape, lambda i: (0,) * a.ndim)
    out_shape = (jax.ShapeDtypeStruct((t, conv_dim), BF16),
                 jax.ShapeDtypeStruct((t, k_w), BF16),
                 jax.ShapeDtypeStruct((t, k_w), BF16),
                 jax.ShapeDtypeStruct((t, kv_lora), F32),
                 jax.ShapeDtypeStruct((t, QK_ROPE), F32),
                 jax.ShapeDtypeStruct((t, LANES), F32),
                 jax.ShapeDtypeStruct((t, conv_dim), F32))
    return pl.pallas_call(
        _proj_decode_kernel, grid=(1,), in_specs=[full(a) for a in args],
        out_specs=tuple(pl.BlockSpec(s.shape, lambda i: (0, 0)) for s in out_shape),
        out_shape=out_shape, compiler_params=_params("arbitrary"), name=name)(*args)


def _attn_kernel(q_ref, k_ref, v_ref, g_ref, o_ref, acc_ref, m_ref, *, exp_scale):
    tq = q_ref.shape[1]
    i = pl.program_id(1)
    row = lax.broadcasted_iota(jnp.int32, (tq, tq), 0)
    col = lax.broadcasted_iota(jnp.int32, (tq, tq), 1)
    acc_ref[...] = jnp.zeros(acc_ref.shape, F32)
    m_ref[...] = jnp.full(m_ref.shape, -jnp.inf, F32)

    def step(j, masked):
        start = pl.multiple_of(j * tq, tq)
        for g0 in range(0, N_HEADS, ATTN_HEAD_GROUP):
            heads = range(g0, g0 + ATTN_HEAD_GROUP)
            scores = []
            for hd in heads:
                lanes = slice(hd * LANES, (hd + 1) * LANES)
                s = lax.dot_general(q_ref[0, :, lanes], k_ref[0, pl.ds(start, tq), lanes],
                                    (((1,), (1,)), ((), ())), preferred_element_type=F32)
                if masked:
                    s = jnp.where(col <= row, s, -jnp.inf)
                scores.append(s)
            for hd, s in zip(heads, scores):
                lanes = slice(hd * LANES, (hd + 1) * LANES)
                m_old = m_ref[hd]
                m_new = jnp.maximum(m_old, jnp.max(s, axis=-1, keepdims=True))
                p = jnp.exp2((s - jnp.concatenate([m_new] * (tq // LANES), axis=-1)) * exp_scale)
                alpha = jnp.exp2((m_old - m_new) * exp_scale)
                acc_ref[hd] = alpha * acc_ref[hd] + jnp.dot(
                    p.astype(BF16), v_ref[0, pl.ds(start, tq), lanes], preferred_element_type=F32)
                m_ref[hd] = m_new

    def body(j, c):
        step(j, False)
        return c
    lax.fori_loop(0, i, body, 0)
    step(i, True)

    lane = lax.broadcasted_iota(jnp.int32, (tq, LANES), 1)
    outs = []
    for pr in range(N_HEADS // 2):
        a, b = acc_ref[2 * pr], acc_ref[2 * pr + 1]
        oa = a * (1.0 / a[:, V_HEAD:V_HEAD + 1])
        ob = b * (1.0 / b[:, 0:1])
        outs.append(jnp.where(lane < V_HEAD, oa, ob))
    o_ref[0] = _rms(jnp.concatenate(outs, axis=-1), g_ref[...]).astype(BF16)


def _attn_call(q, k, v, gn_attn, name):
    b, s, w = q.shape
    tq = min(ATTN_TILE, s)
    assert s % tq == 0 and 2 * V_HEAD == LANES and N_HEADS % 2 == 0
    exp_scale = float((QK_NOPE + QK_ROPE) ** -0.5 * LOG2_E)
    return pl.pallas_call(
        functools.partial(_attn_kernel, exp_scale=exp_scale),
        grid=(b, s // tq),
        in_specs=[pl.BlockSpec((1, tq, w), lambda bi, i: (bi, i, 0)),
                  pl.BlockSpec((1, s, w), lambda bi, i: (bi, 0, 0)),
                  pl.BlockSpec((1, s, w), lambda bi, i: (bi, 0, 0)),
                  _const_spec(gn_attn.shape)],
        out_specs=pl.BlockSpec((1, tq, N_HEADS * V_HEAD), lambda bi, i: (bi, i, 0)),
        out_shape=jax.ShapeDtypeStruct((b, s, N_HEADS * V_HEAD), BF16),
        scratch_shapes=[pltpu.VMEM((N_HEADS, tq, LANES), F32),
                        pltpu.VMEM((N_HEADS, tq, LANES), F32)],
        compiler_params=_params("parallel", "arbitrary"), name=name)(q, k, v, gn_attn)


def _decode_attn_kernel(pt_ref, qlat_ref, q_ref, ckvn_ref, kpen_ref, cckv_hbm, ckpe_hbm,
                        o_ref, ckv_buf, kpe_buf, sem, *, layer, scale):
    g = pl.program_id(0)
    ng = pl.num_programs(0)
    _, rows, n_pages, page_size, _ = ckv_buf.shape
    slot = g % 2

    def copies(gg, sl, r, p):
        page = pt_ref[gg * rows + r, p]
        lane0 = pl.multiple_of(p * page_size, page_size)
        return (pltpu.make_async_copy(cckv_hbm.at[layer, page], ckv_buf.at[sl, r, p], sem.at[0, sl]),
                pltpu.make_async_copy(ckpe_hbm.at[layer, page],
                                      kpe_buf.at[sl, r, :, pl.ds(lane0, page_size)], sem.at[1, sl]))

    def start_fetch(gg, sl):
        def body(p, c):
            for r in range(rows):
                for cp in copies(gg, sl, r, p):
                    cp.start()
            return c
        lax.fori_loop(0, n_pages, body, 0, unroll=DMA_ISSUE_UNROLL)

    @pl.when(g == 0)
    def _():
        start_fetch(0, 0)

    @pl.when(g + 1 < ng)
    def _():
        start_fetch(g + 1, 1 - slot)

    for r in range(rows):
        pltpu.make_async_copy(cckv_hbm.at[layer, pl.ds(0, n_pages)], ckv_buf.at[slot, r],
                              sem.at[0, slot]).wait()
        pltpu.make_async_copy(kpe_buf.at[1 - slot, r], kpe_buf.at[slot, r], sem.at[1, slot]).wait()

    for r in range(rows):
        q_lat = qlat_ref[r]
        q_slab = q_ref[r]
        q_pe = q_slab[:, ROPE_LO:ROPE_LO + QK_ROPE]
        ckv = ckv_buf[slot, r]
        ckv = ckv.reshape(n_pages * page_size, ckv.shape[2]).astype(BF16)
        s_rope = jnp.dot(q_pe, kpe_buf[slot, r].astype(BF16), preferred_element_type=F32)
        s = (lax.dot_general(q_lat, ckv, (((1,), (1,)), ((), ())), preferred_element_type=F32)
             + s_rope) * scale
        ckv_new = ckvn_ref[r]
        s_new = (jnp.sum(q_lat.astype(F32) * ckv_new, axis=-1, keepdims=True)
                 + jnp.sum(q_slab.astype(F32) * kpen_ref[r], axis=-1, keepdims=True)) * scale
        m = jnp.maximum(jnp.max(s, axis=-1, keepdims=True), s_new)
        p = jnp.exp(s - m)
        p_new = jnp.exp(s_new - m)
        l = jnp.sum(p, axis=-1, keepdims=True) + p_new
        o = jnp.dot(p.astype(BF16), ckv, preferred_element_type=F32) + p_new * ckv_new
        o_ref[r] = o / l


def _decode_attn_call(page_table, qlat, q, ckv_new, kpe_new_slab, cache_ckv, cache_kpe_t, layer,
                      name):
    nb, n_pages = page_table.shape
    page, kv_lora = cache_ckv.shape[2], cache_ckv.shape[3]
    assert kv_lora == LANES and cache_kpe_t.shape[2:] == (QK_ROPE, page)
    rows = DECODE_ROWS if nb % DECODE_ROWS == 0 else 1
    heads3 = lambda a: a.reshape(nb, N_HEADS, LANES)
    one3 = lambda a: a.reshape(nb, 1, LANES)
    per_b = lambda sub: pl.BlockSpec((rows, sub, LANES), lambda g, pt: (g, 0, 0))
    grid_spec = pltpu.PrefetchScalarGridSpec(
        num_scalar_prefetch=1, grid=(nb // rows,),
        in_specs=[per_b(N_HEADS), per_b(N_HEADS), per_b(1), per_b(1),
                  pl.BlockSpec(memory_space=pl.ANY), pl.BlockSpec(memory_space=pl.ANY)],
        out_specs=per_b(N_HEADS),
        scratch_shapes=[pltpu.VMEM((2, rows, n_pages, page, kv_lora), F32),
                        pltpu.VMEM((2, rows, QK_ROPE, n_pages * page), F32),
                        pltpu.SemaphoreType.DMA((2, 2))])
    scale = float((QK_NOPE + QK_ROPE) ** -0.5)
    out = pl.pallas_call(
        functools.partial(_decode_attn_kernel, layer=layer, scale=scale),
        grid_spec=grid_spec,
        out_shape=jax.ShapeDtypeStruct((nb, N_HEADS, LANES), F32),
        compiler_params=_params("arbitrary"), name=name)(
            page_table, heads3(qlat), heads3(q), one3(ckv_new), one3(kpe_new_slab),
            cache_ckv, cache_kpe_t)
    return out.reshape(nb, N_HEADS * LANES)


def _rope_tables(pos):
    inv_freq = 1.0 / (ROPE_THETA ** (jnp.arange(HALF_ROPE, dtype=F32) / HALF_ROPE))
    ang = pos.astype(F32)[:, None] * inv_freq[None, :]
    cos, sin = jnp.cos(ang), jnp.sin(ang)
    n = pos.shape[0]
    one = jnp.ones((n, ROPE_LO), F32)
    zero = lambda w: jnp.zeros((n, w), F32)
    tail = LANES - ROPE_LO - QK_ROPE
    c = jnp.concatenate([one, cos, cos, zero(tail)], axis=-1)
    s1 = jnp.concatenate([zero(ROPE_LO), -sin, zero(HALF_ROPE + tail)], axis=-1)
    s2 = jnp.concatenate([zero(ROPE_LO + HALF_ROPE), sin, zero(tail)], axis=-1)
    return jnp.stack([c, s1, s2])


def _layer_weights(l, norm_mix, w_in, conv_w, q_a_norm, w_q_b, kv_a_norm, w_kv_b, gn_conv,
                   gn_attn, w_o):
    d = w_in.shape[1]
    conv_dim, q_lora, kv_lora = conv_w.shape[2], q_a_norm.shape[1], kv_a_norm.shape[1]
    c1 = 3 * conv_dim + q_lora + kv_lora
    tail = LANES - ROPE_LO - QK_ROPE
    win = w_in[l]
    win_ext = jnp.concatenate(
        [win[:, :c1], jnp.zeros((d, ROPE_LO), F32), win[:, c1:], jnp.zeros((d, tail), F32)],
        axis=-1).astype(BF16)
    wq = w_q_b[l].reshape(q_lora, N_HEADS, QK_NOPE + QK_ROPE)
    wq_slab = jnp.concatenate([wq, jnp.zeros((q_lora, N_HEADS, tail), F32)], axis=-1)
    wq_slab = wq_slab.reshape(q_lora, N_HEADS * LANES).astype(BF16)
    wkv = w_kv_b[l].reshape(kv_lora, N_HEADS, QK_NOPE + V_HEAD)
    w_uk, w_uv = wkv[..., :QK_NOPE], wkv[..., QK_NOPE:]
    wk_slab = jnp.concatenate([w_uk, jnp.zeros((kv_lora, N_HEADS, LANES - QK_NOPE), F32)], axis=-1)
    pad = jnp.zeros((kv_lora, N_HEADS // 2, LANES - V_HEAD), F32)
    wv_slab = jnp.stack([jnp.concatenate([w_uv[:, 0::2], pad], axis=-1),
                         jnp.concatenate([pad, w_uv[:, 1::2]], axis=-1)], axis=2)
    wkv_ext = jnp.concatenate([wk_slab.reshape(kv_lora, N_HEADS * LANES),
                               wv_slab.reshape(kv_lora, N_HEADS * LANES)], axis=-1).astype(BF16)
    spare = jnp.zeros((LANES - V_HEAD,), F32).at[0].set(1.0)
    v_one = jnp.tile(jnp.concatenate([jnp.zeros((V_HEAD,), F32), spare, spare,
                                      jnp.zeros((V_HEAD,), F32)]), N_HEADS // 2)[None]
    eye = jnp.eye(N_HEADS, dtype=F32)
    uk_rows = jnp.concatenate(
        [jnp.transpose(w_uk, (1, 2, 0)), jnp.zeros((N_HEADS, LANES - QK_NOPE, kv_lora), F32)],
        axis=1)
    w_uk_bd = (uk_rows[:, :, None, :] * eye[:, None, :, None]).reshape(
        N_HEADS * LANES, N_HEADS * kv_lora).astype(BF16)
    w_uv_bd = (jnp.transpose(w_uv, (1, 0, 2))[:, :, None, :] * eye[:, None, :, None]).reshape(
        N_HEADS * kv_lora, N_HEADS * V_HEAD).astype(BF16)
    return dict(norm_mix=norm_mix[l][None], w_in=win_ext, q_a_norm=q_a_norm[l][None], w_q=wq_slab,
                kv_a_norm=kv_a_norm[l][None], w_kv=wkv_ext, v_one=v_one, w_uk_bd=w_uk_bd,
                w_uv_bd=w_uv_bd, conv_w=conv_w[l], gn_conv=gn_conv[l][None],
                gn_attn=gn_attn[l][None], w_o=w_o[l].astype(BF16))


def kernel(x_prompt, x_sample, state_conv, cache_ckv, cache_kpe, page_table, norm_ffn1, w_ffn1_gate, w_ffn1_up, w_ffn1_down, norm_mix, w_in, conv_w, q_a_norm, w_q_b, kv_a_norm, w_kv_b, gn_conv, gn_attn, w_o, norm_ffn2, w_ffn2_gate, w_ffn2_up, w_ffn2_down, final_norm):
    batch, seq, d = x_prompt.shape
    dec_batch, dec_seq, _ = x_sample.shape
    assert dec_seq == 1
    depth = norm_mix.shape[0]
    past_len = page_table.shape[1] * cache_ckv.shape[2]
    kv_lora = kv_a_norm.shape[1]

    tab_p = _rope_tables(jnp.arange(seq, dtype=jnp.int32))
    tab_s = jnp.broadcast_to(_rope_tables(past_len + jnp.arange(dec_seq, dtype=jnp.int32)),
                             (3, dec_batch, LANES))
    cache_kpe_t = jnp.swapaxes(cache_kpe, 2, 3)
    yp = x_prompt.reshape(batch * seq, d)
    ys = x_sample.reshape(dec_batch, d)
    fin = final_norm[None]
    conv_p, ckv_p, kpe_p, conv_s, ckv_s, kpe_s = [], [], [], [], [], []
    for l in range(depth):
        w = _layer_weights(l, norm_mix, w_in, conv_w, q_a_norm, w_q_b, kv_a_norm, w_kv_b, gn_conv,
                           gn_attn, w_o)
        ffn1 = (norm_ffn1[l][None], w_ffn1_gate[l].astype(BF16), w_ffn1_up[l].astype(BF16),
                w_ffn1_down[l].astype(BF16))
        ffn2 = (norm_ffn2[l][None], w_ffn2_gate[l].astype(BF16), w_ffn2_up[l].astype(BF16),
                w_ffn2_down[l].astype(BF16))
        last = fin if l == depth - 1 else None

        yp = _ffn_call(yp, ffn1, name=f"ffn1_p{l}")
        conv_n, q, k, v, ckv, kpe, new_conv = _proj_prompt_call(yp, w, tab_p, batch, seq,
                                                                name=f"proj_p{l}")
        attn = _attn_call(q.reshape(batch, seq, -1), k.reshape(batch, seq, -1),
                          v.reshape(batch, seq, -1), w["gn_attn"], name=f"attn_p{l}")
        yp = _ffn_call(yp, ffn2, mix_in=(conv_n, attn.reshape(batch * seq, -1), w["w_o"]),
                       final_g=last, name=f"mix_ffn2_p{l}")
        conv_p.append(new_conv)
        ckv_p.append(ckv.reshape(batch, seq, kv_lora))
        kpe_p.append(kpe.reshape(batch, seq, QK_ROPE))

        ys = _ffn_call(ys, ffn1, name=f"ffn1_s{l}")
        s0, s1 = state_conv[l][:, 0, :], state_conv[l][:, 1, :]
        conv_n, qlat, q, ckv, kpe, kpe_slab, u = _proj_decode_call(ys, w, tab_s, s0, s1,
                                                                   name=f"proj_s{l}")
        o_lat = _decode_attn_call(page_table, qlat, q, ckv, kpe_slab, cache_ckv, cache_kpe_t, l,
                                  name=f"attn_s{l}")
        ys = _ffn_call(ys, ffn2, mix_in=(conv_n, o_lat, w["w_o"]),
                       decode_in=(w["w_uv_bd"], w["gn_attn"]), final_g=last,
                       name=f"mix_ffn2_s{l}")
        conv_s.append(jnp.stack([s1, u], axis=1))
        ckv_s.append(ckv.reshape(dec_batch, dec_seq, kv_lora))
        kpe_s.append(kpe.reshape(dec_batch, dec_seq, QK_ROPE))

    return (yp.reshape(batch, seq, d), ys.reshape(dec_batch, dec_seq, d),
            jnp.stack(conv_p), jnp.stack(ckv_p), jnp.stack(kpe_p),
            jnp.stack(conv_s), jnp.stack(ckv_s), jnp.stack(kpe_s))
```

```python
import functools

import jax
import jax.numpy as jnp
from jax import lax
from jax.experimental import pallas as pl
from jax.experimental.pallas import tpu as pltpu

N_HEADS = 8
QK_NOPE = 64
QK_ROPE = 32
V_HEAD = 64
CONV_WIDTH = 3
ROPE_THETA = 10000.0
RMS_EPS = 1e-6

LANES = 128
SUBLANES = 8
MXU_COLS = 256
VMEM_LIMIT_BYTES = 56 * 2**20

ROW_TILE = 512
PROJ_SUBTILES = 2
ATTN_Q_TILE = 512
ATTN_K_TILE = 256
ATTN_LOOP_GROUP = 1
ATTN_DIAG_GROUP = 4
DMA_ISSUE_UNROLL = 8
DECODE_ROWS = 2
HALF_ROPE = QK_ROPE // 2
ROPE_LO = QK_NOPE
LOG2_E = 1.4426950408889634

F32 = jnp.float32
BF16 = jnp.bfloat16


def _const_spec(shape):
    return pl.BlockSpec(shape, lambda *_: (0,) * len(shape), pipeline_mode=pl.Buffered(1))


def _layer_spec(stacked, layer):
    _, r, c = stacked.shape
    return pl.BlockSpec((None, r, c), lambda *_: (layer, 0, 0), pipeline_mode=pl.Buffered(1))


def _params(*semantics):
    return pltpu.CompilerParams(dimension_semantics=semantics, vmem_limit_bytes=VMEM_LIMIT_BYTES)


def _cast_kernel(x_ref, o_ref):
    o_ref[...] = x_ref[...].astype(BF16)


def _cast_call(w, name):
    depth, r, c = w.shape
    pack = 2 * SUBLANES
    tr = max(t for t in range(pack, min(r, ROW_TILE) + 1, pack) if r % t == 0)
    spec = pl.BlockSpec((1, tr, c), lambda l, i: (l, i, 0))
    return pl.pallas_call(
        _cast_kernel, grid=(depth, r // tr), in_specs=[spec], out_specs=spec,
        out_shape=jax.ShapeDtypeStruct(w.shape, BF16),
        compiler_params=_params("parallel", "parallel"), name=name)(w)


def _rms(x, g):
    return x * lax.rsqrt(jnp.mean(x * x, axis=-1, keepdims=True) + RMS_EPS) * g


def _ff_chunks(d_ff):
    half = -(-(d_ff // 2) // MXU_COLS) * MXU_COLS
    return ((0, half), (half, d_ff)) if 0 < half < d_ff else ((0, d_ff),)


def _swiglu_residual(x, g_ref, wg_ref, wu_ref, wd_ref):
    xn = _rms(x, g_ref[...]).astype(BF16)
    out = None
    for f0, f1 in _ff_chunks(wg_ref.shape[1]):
        hg = jnp.dot(xn, wg_ref[:, f0:f1], preferred_element_type=F32)
        hu = jnp.dot(xn, wu_ref[:, f0:f1], preferred_element_type=F32)
        a = (hg * jax.nn.sigmoid(hg) * hu).astype(BF16)
        o = jnp.dot(a, wd_ref[f0:f1, :], preferred_element_type=F32)
        out = o if out is None else out + o
    return x + 0.5 * out


def _ffn_kernel(*refs, mix, decode, final):
    refs = list(refs)
    o_ref = refs.pop()
    x = refs.pop(0)[...]
    if mix:
        conv_ref, attn_ref, wo_ref = refs[:3]
        refs = refs[3:]
        attn = attn_ref[...]
        if decode:
            wuv_ref, gn_attn_ref = refs[:2]
            refs = refs[2:]
            attn = jnp.dot(attn.astype(BF16), wuv_ref[...], preferred_element_type=F32)
            attn = _rms(attn, gn_attn_ref[...]).astype(BF16)
        merged = jnp.concatenate([conv_ref[...], attn], axis=-1)
        x = x + jnp.dot(merged, wo_ref[...], preferred_element_type=F32)
    g_ref, wg_ref, wu_ref, wd_ref = refs[:4]
    y = _swiglu_residual(x, g_ref, wg_ref, wu_ref, wd_ref)
    if final:
        y = _rms(y, refs[4][...])
    o_ref[...] = y


def _ffn_call(x, layer, ffn_w, *, mix_in=None, decode_in=None, final_g=None, name):
    t, d = x.shape
    tm = min(ROW_TILE, t)
    assert t % tm == 0
    row = lambda w: pl.BlockSpec((tm, w), lambda i: (i, 0))
    args, specs = [x], [row(d)]
    if mix_in is not None:
        conv_n, attn, w_o = mix_in
        args += [conv_n, attn, w_o]
        specs += [row(conv_n.shape[1]), row(attn.shape[1]), _layer_spec(w_o, layer)]
        if decode_in is not None:
            args += list(decode_in)
            specs += [_const_spec(a.shape) for a in decode_in]
    args += list(ffn_w)
    specs += [_const_spec(ffn_w[0].shape)] + [_layer_spec(w, layer) for w in ffn_w[1:]]
    if final_g is not None:
        args.append(final_g)
        specs.append(_const_spec(final_g.shape))
    kern = functools.partial(_ffn_kernel, mix=mix_in is not None, decode=decode_in is not None,
                             final=final_g is not None)
    return pl.pallas_call(
        kern, grid=(t // tm,), in_specs=specs, out_specs=row(d),
        out_shape=jax.ShapeDtypeStruct((t, d), F32),
        compiler_params=_params("parallel"), name=name)(*args)


def _rope_slab(x, tab):
    return (x * tab[0]
            + pltpu.roll(x, LANES - HALF_ROPE, 1) * tab[1]
            + pltpu.roll(x, HALF_ROPE, 1) * tab[2])


def _proj_common(x, tab, gmix_ref, win_ref, gq_ref, wq_ref, gkv_ref, conv_dim):
    q_lora, kv_lora = gq_ref.shape[1], gkv_ref.shape[1]
    h = _rms(x, gmix_ref[...]).astype(BF16)
    proj = jnp.dot(h, win_ref[...], preferred_element_type=F32)
    c0 = 3 * conv_dim
    b_g = proj[:, :conv_dim]
    u = proj[:, conv_dim:2 * conv_dim] * proj[:, 2 * conv_dim:c0]
    q_c = proj[:, c0:c0 + q_lora]
    kv_c = proj[:, c0 + q_lora:c0 + q_lora + kv_lora]
    kpe_slab = proj[:, c0 + q_lora + kv_lora:]
    q = jnp.dot(_rms(q_c, gq_ref[...]).astype(BF16), wq_ref[...], preferred_element_type=F32)
    q = jnp.concatenate(
        [_rope_slab(q[:, hd * LANES:(hd + 1) * LANES], tab) for hd in range(N_HEADS)], axis=-1)
    ckv = _rms(kv_c, gkv_ref[...])
    kpe_slab = _rope_slab(kpe_slab, tab)
    return b_g, u, q, ckv, kpe_slab


def _proj_prompt_kernel(x_ref, gmix_ref, win_ref, gq_ref, wq_ref, gkv_ref, wkv_ref, vone_ref,
                        convw_ref, gconv_ref, tab_ref,
                        convn_ref, q_ref, k_ref, v_ref, ckv_ref, kpe_ref, newconv_ref,
                        uext_ref, *, tiles_per_seq):
    tm = x_ref.shape[0]
    conv_dim = convw_ref.shape[1]
    k_w = N_HEADS * LANES

    @pl.when(pl.program_id(0) % tiles_per_seq == 0)
    def _():
        uext_ref[0:SUBLANES, :] = jnp.zeros((SUBLANES, conv_dim), F32)

    sub = tm // PROJ_SUBTILES
    for r0 in range(0, tm, sub):
        rows = slice(r0, r0 + sub)
        b_g, u, q, ckv, kpe_slab = _proj_common(
            x_ref[rows, :], tab_ref[:, rows, :], gmix_ref, win_ref, gq_ref, wq_ref, gkv_ref,
            conv_dim)
        uext_ref[SUBLANES + r0:SUBLANES + r0 + sub, :] = u
        u1 = uext_ref[SUBLANES - 1 + r0:SUBLANES - 1 + r0 + sub, :]
        u2 = uext_ref[SUBLANES - 2 + r0:SUBLANES - 2 + r0 + sub, :]
        conv_y = b_g * (convw_ref[0:1, :] * u2 + convw_ref[1:2, :] * u1 + convw_ref[2:3, :] * u)
        convn_ref[rows, :] = _rms(conv_y, gconv_ref[...]).astype(BF16)
        q_ref[rows, :] = q.astype(BF16)
        ckv_ref[rows, :] = ckv
        kpe_ref[rows, :] = kpe_slab[:, ROPE_LO:ROPE_LO + QK_ROPE]
        kv = jnp.dot(ckv.astype(BF16), wkv_ref[...], preferred_element_type=F32)
        k_ref[rows, :] = (kv[:, :k_w]
                          + jnp.concatenate([kpe_slab] * N_HEADS, axis=-1)).astype(BF16)
        v_ref[rows, :] = (kv[:, k_w:] + vone_ref[...]).astype(BF16)
    newconv_ref[0] = u[sub - (CONV_WIDTH - 1):, :]
    uext_ref[0:SUBLANES, :] = u[sub - SUBLANES:, :]


def _proj_decode_kernel(x_ref, gmix_ref, win_ref, gq_ref, wq_ref, gkv_ref, wuk_ref, convw_ref,
                        gconv_ref, tab_ref, s0_ref, s1_ref,
                        convn_ref, qlat_ref, q_ref, ckv_ref, kpe_ref, kpeslab_ref, u_ref):
    conv_dim = convw_ref.shape[1]
    b_g, u, q, ckv, kpe_slab = _proj_common(
        x_ref[...], tab_ref[...], gmix_ref, win_ref, gq_ref, wq_ref, gkv_ref, conv_dim)
    conv_y = b_g * (convw_ref[0:1, :] * s0_ref[...] + convw_ref[1:2, :] * s1_ref[...]
                    + convw_ref[2:3, :] * u)
    convn_ref[...] = _rms(conv_y, gconv_ref[...]).astype(BF16)
    u_ref[...] = u
    qb = q.astype(BF16)
    q_ref[...] = qb
    qlat_ref[...] = jnp.dot(qb, wuk_ref[...], preferred_element_type=F32).astype(BF16)
    ckv_ref[...] = ckv
    kpe_ref[...] = kpe_slab[:, ROPE_LO:ROPE_LO + QK_ROPE]
    kpeslab_ref[...] = kpe_slab


def _proj_prompt_call(x, w, tab, batch, seq, name):
    t, d = x.shape
    tm = min(ROW_TILE, seq)
    assert seq % tm == 0 and tm >= SUBLANES
    tiles_per_seq = seq // tm
    conv_dim = w["conv_w"].shape[1]
    kv_lora = w["kv_a_norm"].shape[1]
    k_w = N_HEADS * LANES
    row = lambda width: pl.BlockSpec((tm, width), lambda i: (i, 0))
    consts = [w["norm_mix"], w["w_in"], w["q_a_norm"], w["w_q"], w["kv_a_norm"], w["w_kv"],
              w["v_one"], w["conv_w"], w["gn_conv"]]
    in_specs = ([row(d)] + [_const_spec(c.shape) for c in consts]
                + [pl.BlockSpec((3, tm, LANES), lambda i: (0, i % tiles_per_seq, 0))])
    out_shape = (jax.ShapeDtypeStruct((t, conv_dim), BF16),
                 jax.ShapeDtypeStruct((t, k_w), BF16),
                 jax.ShapeDtypeStruct((t, k_w), BF16),
                 jax.ShapeDtypeStruct((t, k_w), BF16),
                 jax.ShapeDtypeStruct((t, kv_lora), F32),
                 jax.ShapeDtypeStruct((t, QK_ROPE), F32),
                 jax.ShapeDtypeStruct((batch, CONV_WIDTH - 1, conv_dim), F32))
    out_specs = (row(conv_dim), row(k_w), row(k_w), row(k_w), row(kv_lora), row(QK_ROPE),
                 pl.BlockSpec((1, CONV_WIDTH - 1, conv_dim), lambda i: (i // tiles_per_seq, 0, 0)))
    return pl.pallas_call(
        functools.partial(_proj_prompt_kernel, tiles_per_seq=tiles_per_seq),
        grid=(t // tm,), in_specs=in_specs, out_specs=out_specs, out_shape=out_shape,
        scratch_shapes=[pltpu.VMEM((tm + SUBLANES, conv_dim), F32)],
        compiler_params=_params("arbitrary"), name=name)(x, *consts, tab)


def _proj_decode_call(x, w, tab, s0, s1, name):
    t, d = x.shape
    conv_dim = w["conv_w"].shape[1]
    kv_lora = w["kv_a_norm"].shape[1]
    k_w = N_HEADS * LANES
    args = [x, w["norm_mix"], w["w_in"], w["q_a_norm"], w["w_q"], w["kv_a_norm"], w["w_uk_bd"],
            w["conv_w"], w["gn_conv"], tab, s0, s1]
    full = lambda a: pl.BlockSpec(a.shape, lambda i: (0,) * a.ndim)
    out_shape = (jax.ShapeDtypeStruct((t, conv_dim), BF16),
                 jax.ShapeDtypeStruct((t, k_w), BF16),
                 jax.ShapeDtypeStruct((t, k_w), BF16),
                 jax.ShapeDtypeStruct((t, kv_lora), F32),
                 jax.ShapeDtypeStruct((t, QK_ROPE), F32),
                 jax.ShapeDtypeStruct((t, LANES), F32),
                 jax.ShapeDtypeStruct((t, conv_dim), F32))
    return pl.pallas_call(
        _proj_decode_kernel, grid=(1,), in_specs=[full(a) for a in args],
        out_specs=tuple(pl.BlockSpec(s.shape, lambda i: (0, 0)) for s in out_shape),
        out_shape=out_shape, compiler_params=_params("arbitrary"), name=name)(*args)


def _attn_kernel(q_ref, k_ref, v_ref, g_ref, o_ref, acc_ref, m_ref, *, exp_scale):
    tq = q_ref.shape[1]
    tk = ATTN_K_TILE if tq % ATTN_K_TILE == 0 else tq
    blocks_per_tile = tq // tk
    i = pl.program_id(1)
    acc_ref[...] = jnp.zeros(acc_ref.shape, F32)
    m_ref[...] = jnp.full(m_ref.shape, -jnp.inf, F32)

    def block(j, r0, diag, group):
        rows = slice(r0, tq)
        start = pl.multiple_of(j * tk, tk)
        for g0 in range(0, N_HEADS, group):
            heads = range(g0, g0 + group)
            scores = []
            for hd in heads:
                lanes = slice(hd * LANES, (hd + 1) * LANES)
                s = lax.dot_general(q_ref[0, rows, lanes], k_ref[0, pl.ds(start, tk), lanes],
                                    (((1,), (1,)), ((), ())), preferred_element_type=F32)
                if diag:
                    row = lax.broadcasted_iota(jnp.int32, s.shape, 0)
                    col = lax.broadcasted_iota(jnp.int32, s.shape, 1)
                    s = jnp.where(col <= row, s, -jnp.inf)
                scores.append(s)
            for hd, s in zip(heads, scores):
                lanes = slice(hd * LANES, (hd + 1) * LANES)
                m_old = m_ref[hd, rows, :]
                m_new = jnp.maximum(m_old, jnp.max(s, axis=-1, keepdims=True))
                p = jnp.exp2((s - jnp.concatenate([m_new] * (tk // LANES), axis=-1)) * exp_scale)
                alpha = jnp.exp2((m_old - m_new) * exp_scale)
                acc_ref[hd, rows, :] = alpha * acc_ref[hd, rows, :] + jnp.dot(
                    p.astype(BF16), v_ref[0, pl.ds(start, tk), lanes], preferred_element_type=F32)
                m_ref[hd, rows, :] = m_new

    def body(j, c):
        block(j, 0, False, ATTN_LOOP_GROUP)
        return c
    lax.fori_loop(0, i * blocks_per_tile, body, 0)
    for d in range(blocks_per_tile):
        block(i * blocks_per_tile + d, d * tk, True, ATTN_DIAG_GROUP)

    lane = lax.broadcasted_iota(jnp.int32, (tq, LANES), 1)
    outs = []
    for pr in range(N_HEADS // 2):
        a, b = acc_ref[2 * pr], acc_ref[2 * pr + 1]
        oa = a * (1.0 / a[:, V_HEAD:V_HEAD + 1])
        ob = b * (1.0 / b[:, 0:1])
        outs.append(jnp.where(lane < V_HEAD, oa, ob))
    o_ref[0] = _rms(jnp.concatenate(outs, axis=-1), g_ref[...]).astype(BF16)


def _attn_call(q, k, v, gn_attn, name):
    b, s, w = q.shape
    tq = min(ATTN_Q_TILE, s)
    assert s % tq == 0 and 2 * V_HEAD == LANES and N_HEADS % 2 == 0
    exp_scale = float((QK_NOPE + QK_ROPE) ** -0.5 * LOG2_E)
    return pl.pallas_call(
        functools.partial(_attn_kernel, exp_scale=exp_scale),
        grid=(b, s // tq),
        in_specs=[pl.BlockSpec((1, tq, w), lambda bi, i: (bi, i, 0)),
                  pl.BlockSpec((1, s, w), lambda bi, i: (bi, 0, 0)),
                  pl.BlockSpec((1, s, w), lambda bi, i: (bi, 0, 0)),
                  _const_spec(gn_attn.shape)],
        out_specs=pl.BlockSpec((1, tq, N_HEADS * V_HEAD), lambda bi, i: (bi, i, 0)),
        out_shape=jax.ShapeDtypeStruct((b, s, N_HEADS * V_HEAD), BF16),
        scratch_shapes=[pltpu.VMEM((N_HEADS, tq, LANES), F32),
                        pltpu.VMEM((N_HEADS, tq, LANES), F32)],
        compiler_params=_params("parallel", "arbitrary"), name=name)(q, k, v, gn_attn)


def _decode_attn_kernel(pt_ref, qlat_ref, q_ref, ckvn_ref, kpen_ref, cckv_hbm, ckpe_hbm,
                        o_ref, ckv_buf, kpe_buf, sem, *, layer, scale):
    g = pl.program_id(0)
    ng = pl.num_programs(0)
    _, rows, n_pages, page_size, _ = ckv_buf.shape
    slot = g % 2

    def copies(gg, sl, r, p):
        page = pt_ref[gg * rows + r, p]
        lane0 = pl.multiple_of(p * page_size, page_size)
        return (pltpu.make_async_copy(cckv_hbm.at[layer, page], ckv_buf.at[sl, r, p], sem.at[0, sl]),
                pltpu.make_async_copy(ckpe_hbm.at[layer, page],
                                      kpe_buf.at[sl, r, :, pl.ds(lane0, page_size)], sem.at[1, sl]))

    def start_fetch(gg, sl):
        def body(p, c):
            for r in range(rows):
                for cp in copies(gg, sl, r, p):
                    cp.start()
            return c
        lax.fori_loop(0, n_pages, body, 0, unroll=DMA_ISSUE_UNROLL)

    @pl.when(g == 0)
    def _():
        start_fetch(0, 0)

    @pl.when(g + 1 < ng)
    def _():
        start_fetch(g + 1, 1 - slot)

    for r in range(rows):
        pltpu.make_async_copy(cckv_hbm.at[layer, pl.ds(0, n_pages)], ckv_buf.at[slot, r],
                              sem.at[0, slot]).wait()
        pltpu.make_async_copy(kpe_buf.at[1 - slot, r], kpe_buf.at[slot, r], sem.at[1, slot]).wait()

    for r in range(rows):
        q_lat = qlat_ref[r]
        q_slab = q_ref[r]
        q_pe = q_slab[:, ROPE_LO:ROPE_LO + QK_ROPE]
        ckv = ckv_buf[slot, r]
        ckv = ckv.reshape(n_pages * page_size, ckv.shape[2]).astype(BF16)
        s_rope = jnp.dot(q_pe, kpe_buf[slot, r].astype(BF16), preferred_element_type=F32)
        s = (lax.dot_general(q_lat, ckv, (((1,), (1,)), ((), ())), preferred_element_type=F32)
             + s_rope) * scale
        ckv_new = ckvn_ref[r]
        s_new = (jnp.sum(q_lat.astype(F32) * ckv_new, axis=-1, keepdims=True)
                 + jnp.sum(q_slab.astype(F32) * kpen_ref[r], axis=-1, keepdims=True)) * scale
        m = jnp.maximum(jnp.max(s, axis=-1, keepdims=True), s_new)
        p = jnp.exp(s - m)
        p_new = jnp.exp(s_new - m)
        l = jnp.sum(p, axis=-1, keepdims=True) + p_new
        o = jnp.dot(p.astype(BF16), ckv, preferred_element_type=F32) + p_new * ckv_new
        o_ref[r] = o / l


def _decode_attn_call(page_table, qlat, q, ckv_new, kpe_new_slab, cache_ckv, cache_kpe_t, layer,
                      name):
    nb, n_pages = page_table.shape
    page, kv_lora = cache_ckv.shape[2], cache_ckv.shape[3]
    assert kv_lora == LANES and cache_kpe_t.shape[2:] == (QK_ROPE, page)
    rows = DECODE_ROWS if nb % DECODE_ROWS == 0 else 1
    heads3 = lambda a: a.reshape(nb, N_HEADS, LANES)
    one3 = lambda a: a.reshape(nb, 1, LANES)
    per_b = lambda sub: pl.BlockSpec((rows, sub, LANES), lambda g, pt: (g, 0, 0))
    grid_spec = pltpu.PrefetchScalarGridSpec(
        num_scalar_prefetch=1, grid=(nb // rows,),
        in_specs=[per_b(N_HEADS), per_b(N_HEADS), per_b(1), per_b(1),
                  pl.BlockSpec(memory_space=pl.ANY), pl.BlockSpec(memory_space=pl.ANY)],
        out_specs=per_b(N_HEADS),
        scratch_shapes=[pltpu.VMEM((2, rows, n_pages, page, kv_lora), F32),
                        pltpu.VMEM((2, rows, QK_ROPE, n_pages * page), F32),
                        pltpu.SemaphoreType.DMA((2, 2))])
    scale = float((QK_NOPE + QK_ROPE) ** -0.5)
    out = pl.pallas_call(
        functools.partial(_decode_attn_kernel, layer=layer, scale=scale),
        grid_spec=grid_spec,
        out_shape=jax.ShapeDtypeStruct((nb, N_HEADS, LANES), F32),
        compiler_params=_params("arbitrary"), name=name)(
            page_table, heads3(qlat), heads3(q), one3(ckv_new), one3(kpe_new_slab),
            cache_ckv, cache_kpe_t)
    return out.reshape(nb, N_HEADS * LANES)


def _rope_tables(pos):
    inv_freq = 1.0 / (ROPE_THETA ** (jnp.arange(HALF_ROPE, dtype=F32) / HALF_ROPE))
    ang = pos.astype(F32)[:, None] * inv_freq[None, :]
    cos, sin = jnp.cos(ang), jnp.sin(ang)
    n = pos.shape[0]
    one = jnp.ones((n, ROPE_LO), F32)
    zero = lambda w: jnp.zeros((n, w), F32)
    tail = LANES - ROPE_LO - QK_ROPE
    c = jnp.concatenate([one, cos, cos, zero(tail)], axis=-1)
    s1 = jnp.concatenate([zero(ROPE_LO), -sin, zero(HALF_ROPE + tail)], axis=-1)
    s2 = jnp.concatenate([zero(ROPE_LO + HALF_ROPE), sin, zero(tail)], axis=-1)
    return jnp.stack([c, s1, s2])


def _layer_weights(l, norm_mix, w_in, conv_w, q_a_norm, w_q_b, kv_a_norm, w_kv_b, gn_conv,
                   gn_attn):
    d = w_in.shape[1]
    conv_dim, q_lora, kv_lora = conv_w.shape[2], q_a_norm.shape[1], kv_a_norm.shape[1]
    c1 = 3 * conv_dim + q_lora + kv_lora
    tail = LANES - ROPE_LO - QK_ROPE
    win = w_in[l]
    win_ext = jnp.concatenate(
        [win[:, :c1], jnp.zeros((d, ROPE_LO), F32), win[:, c1:], jnp.zeros((d, tail), F32)],
        axis=-1).astype(BF16)
    wq = w_q_b[l].reshape(q_lora, N_HEADS, QK_NOPE + QK_ROPE)
    wq_slab = jnp.concatenate([wq, jnp.zeros((q_lora, N_HEADS, tail), F32)], axis=-1)
    wq_slab = wq_slab.reshape(q_lora, N_HEADS * LANES).astype(BF16)
    wkv = w_kv_b[l].reshape(kv_lora, N_HEADS, QK_NOPE + V_HEAD)
    w_uk, w_uv = wkv[..., :QK_NOPE], wkv[..., QK_NOPE:]
    wk_slab = jnp.concatenate([w_uk, jnp.zeros((kv_lora, N_HEADS, LANES - QK_NOPE), F32)], axis=-1)
    pad = jnp.zeros((kv_lora, N_HEADS // 2, LANES - V_HEAD), F32)
    wv_slab = jnp.stack([jnp.concatenate([w_uv[:, 0::2], pad], axis=-1),
                         jnp.concatenate([pad, w_uv[:, 1::2]], axis=-1)], axis=2)
    wkv_ext = jnp.concatenate([wk_slab.reshape(kv_lora, N_HEADS * LANES),
                               wv_slab.reshape(kv_lora, N_HEADS * LANES)], axis=-1).astype(BF16)
    spare = jnp.zeros((LANES - V_HEAD,), F32).at[0].set(1.0)
    v_one = jnp.tile(jnp.concatenate([jnp.zeros((V_HEAD,), F32), spare, spare,
                                      jnp.zeros((V_HEAD,), F32)]), N_HEADS // 2)[None]
    eye = jnp.eye(N_HEADS, dtype=F32)
    uk_rows = jnp.concatenate(
        [jnp.transpose(w_uk, (1, 2, 0)), jnp.zeros((N_HEADS, LANES - QK_NOPE, kv_lora), F32)],
        axis=1)
    w_uk_bd = (uk_rows[:, :, None, :] * eye[:, None, :, None]).reshape(
        N_HEADS * LANES, N_HEADS * kv_lora).astype(BF16)
    w_uv_bd = (jnp.transpose(w_uv, (1, 0, 2))[:, :, None, :] * eye[:, None, :, None]).reshape(
        N_HEADS * kv_lora, N_HEADS * V_HEAD).astype(BF16)
    return dict(norm_mix=norm_mix[l][None], w_in=win_ext, q_a_norm=q_a_norm[l][None], w_q=wq_slab,
                kv_a_norm=kv_a_norm[l][None], w_kv=wkv_ext, v_one=v_one, w_uk_bd=w_uk_bd,
                w_uv_bd=w_uv_bd, conv_w=conv_w[l], gn_conv=gn_conv[l][None],
                gn_attn=gn_attn[l][None])


def kernel(x_prompt, x_sample, state_conv, cache_ckv, cache_kpe, page_table, norm_ffn1, w_ffn1_gate, w_ffn1_up, w_ffn1_down, norm_mix, w_in, conv_w, q_a_norm, w_q_b, kv_a_norm, w_kv_b, gn_conv, gn_attn, w_o, norm_ffn2, w_ffn2_gate, w_ffn2_up, w_ffn2_down, final_norm):
    batch, seq, d = x_prompt.shape
    dec_batch, dec_seq, _ = x_sample.shape
    assert dec_seq == 1
    depth = norm_mix.shape[0]
    past_len = page_table.shape[1] * cache_ckv.shape[2]
    kv_lora = kv_a_norm.shape[1]

    tab_p = _rope_tables(jnp.arange(seq, dtype=jnp.int32))
    tab_s = jnp.broadcast_to(_rope_tables(past_len + jnp.arange(dec_seq, dtype=jnp.int32)),
                             (3, dec_batch, LANES))
    cache_kpe_t = jnp.swapaxes(cache_kpe, 2, 3)
    yp = x_prompt.reshape(batch * seq, d)
    ys = x_sample.reshape(dec_batch, d)
    fin = final_norm[None]
    ffn1_w = [_cast_call(w, name=f"cast_ffn1_{n}") for n, w in
              enumerate((w_ffn1_gate, w_ffn1_up, w_ffn1_down))]
    ffn2_w = [_cast_call(w, name=f"cast_ffn2_{n}") for n, w in
              enumerate((w_ffn2_gate, w_ffn2_up, w_ffn2_down))]
    w_o_bf = _cast_call(w_o, name="cast_w_o")
    conv_p, ckv_p, kpe_p, conv_s, ckv_s, kpe_s = [], [], [], [], [], []
    for l in range(depth):
        w = _layer_weights(l, norm_mix, w_in, conv_w, q_a_norm, w_q_b, kv_a_norm, w_kv_b, gn_conv,
                           gn_attn)
        ffn1 = (norm_ffn1[l][None], *ffn1_w)
        ffn2 = (norm_ffn2[l][None], *ffn2_w)
        last = fin if l == depth - 1 else None

        yp = _ffn_call(yp, l, ffn1, name=f"ffn1_p{l}")
        conv_n, q, k, v, ckv, kpe, new_conv = _proj_prompt_call(yp, w, tab_p, batch, seq,
                                                                name=f"proj_p{l}")
        attn = _attn_call(q.reshape(batch, seq, -1), k.reshape(batch, seq, -1),
                          v.reshape(batch, seq, -1), w["gn_attn"], name=f"attn_p{l}")
        yp = _ffn_call(yp, l, ffn2, mix_in=(conv_n, attn.reshape(batch * seq, -1), w_o_bf),
                       final_g=last, name=f"mix_ffn2_p{l}")
        conv_p.append(new_conv)
        ckv_p.append(ckv.reshape(batch, seq, kv_lora))
        kpe_p.append(kpe.reshape(batch, seq, QK_ROPE))

        ys = _ffn_call(ys, l, ffn1, name=f"ffn1_s{l}")
        s0, s1 = state_conv[l][:, 0, :], state_conv[l][:, 1, :]
        conv_n, qlat, q, ckv, kpe, kpe_slab, u = _proj_decode_call(ys, w, tab_s, s0, s1,
                                                                   name=f"proj_s{l}")
        o_lat = _decode_attn_call(page_table, qlat, q, ckv, kpe_slab, cache_ckv, cache_kpe_t, l,
                                  name=f"attn_s{l}")
        ys = _ffn_call(ys, l, ffn2, mix_in=(conv_n, o_lat, w_o_bf),
                       decode_in=(w["w_uv_bd"], w["gn_attn"]), final_g=last,
                       name=f"mix_ffn2_s{l}")
        conv_s.append(jnp.stack([s1, u], axis=1))
        ckv_s.append(ckv.reshape(dec_batch, dec_seq, kv_lora))
        kpe_s.append(kpe.reshape(dec_batch, dec_seq, QK_ROPE))

    return (yp.reshape(batch, seq, d), ys.reshape(dec_batch, dec_seq, d),
            jnp.stack(conv_p), jnp.stack(ckv_p), jnp.stack(kpe_p),
            jnp.stack(conv_s), jnp.stack(ckv_s), jnp.stack(kpe_s))
```

```python
import functools

import jax
import jax.numpy as jnp
from jax import lax
from jax.experimental import pallas as pl
from jax.experimental.pallas import tpu as pltpu

N_HEADS = 8
QK_NOPE = 64
QK_ROPE = 32
V_HEAD = 64
CONV_WIDTH = 3
ROPE_THETA = 10000.0
RMS_EPS = 1e-6

LANES = 128
SUBLANES = 8
MXU_COLS = 256
VMEM_LIMIT_BYTES = 56 * 2**20

ROW_TILE = 512
PROJ_ROW_TILE = 1024
PROJ_SUBTILES = 4
CAST_TILE_BYTES = 6 * 2**20
ATTN_Q_TILE = 1024
ATTN_K_TILE = 256
DMA_ISSUE_UNROLL = 8
DECODE_ROWS = 2
HALF_ROPE = QK_ROPE // 2
ROPE_LO = QK_NOPE
LOG2_E = 1.4426950408889634

F32 = jnp.float32
BF16 = jnp.bfloat16


def _const_spec(shape):
    return pl.BlockSpec(shape, lambda *_: (0,) * len(shape), pipeline_mode=pl.Buffered(1))


def _layer_spec(stacked, layer):
    _, r, c = stacked.shape
    return pl.BlockSpec((None, r, c), lambda *_: (layer, 0, 0), pipeline_mode=pl.Buffered(1))


def _params(*semantics):
    return pltpu.CompilerParams(dimension_semantics=semantics, vmem_limit_bytes=VMEM_LIMIT_BYTES)


def _cast_kernel(x_ref, o_ref):
    o_ref[...] = x_ref[...].astype(BF16)


def _cast_call(w, name):
    depth, r, c = w.shape
    pack = 2 * SUBLANES
    cap = max(pack, CAST_TILE_BYTES // (4 * c))
    tr = max(t for t in range(pack, min(r, cap) + 1, pack) if r % t == 0)
    spec = pl.BlockSpec((1, tr, c), lambda l, i: (l, i, 0))
    return pl.pallas_call(
        _cast_kernel, grid=(depth, r // tr), in_specs=[spec], out_specs=spec,
        out_shape=jax.ShapeDtypeStruct(w.shape, BF16),
        compiler_params=_params("parallel", "parallel"), name=name)(w)


def _rms(x, g):
    return x * lax.rsqrt(jnp.mean(x * x, axis=-1, keepdims=True) + RMS_EPS) * g


def _ff_chunks(d_ff):
    half = -(-(d_ff // 2) // MXU_COLS) * MXU_COLS
    return ((0, half), (half, d_ff)) if 0 < half < d_ff else ((0, d_ff),)


def _swiglu_residual(x, g_ref, wg_ref, wu_ref, wd_ref):
    xn = _rms(x, g_ref[...]).astype(BF16)
    out = None
    for f0, f1 in _ff_chunks(wg_ref.shape[1]):
        hg = jnp.dot(xn, wg_ref[:, f0:f1], preferred_element_type=F32)
        hu = jnp.dot(xn, wu_ref[:, f0:f1], preferred_element_type=F32)
        a = (hg * jax.nn.sigmoid(hg) * hu).astype(BF16)
        o = jnp.dot(a, wd_ref[f0:f1, :], preferred_element_type=F32)
        out = o if out is None else out + o
    return x + 0.5 * out


def _ffn_kernel(*refs, mix, decode, final):
    refs = list(refs)
    o_ref = refs.pop()
    x = refs.pop(0)[...]
    if mix:
        conv_ref, attn_ref, wo_ref = refs[:3]
        refs = refs[3:]
        attn = attn_ref[...]
        if decode:
            wuv_ref, gn_attn_ref = refs[:2]
            refs = refs[2:]
            attn = jnp.dot(attn.astype(BF16), wuv_ref[...], preferred_element_type=F32)
            attn = _rms(attn, gn_attn_ref[...]).astype(BF16)
        merged = jnp.concatenate([conv_ref[...], attn], axis=-1)
        x = x + jnp.dot(merged, wo_ref[...], preferred_element_type=F32)
    g_ref, wg_ref, wu_ref, wd_ref = refs[:4]
    y = _swiglu_residual(x, g_ref, wg_ref, wu_ref, wd_ref)
    if final:
        y = _rms(y, refs[4][...])
    o_ref[...] = y


def _ffn_call(x, layer, ffn_w, *, mix_in=None, decode_in=None, final_g=None, name):
    t, d = x.shape
    tm = min(ROW_TILE, t)
    assert t % tm == 0
    row = lambda w: pl.BlockSpec((tm, w), lambda i: (i, 0))
    args, specs = [x], [row(d)]
    if mix_in is not None:
        conv_n, attn, w_o = mix_in
        args += [conv_n, attn, w_o]
        specs += [row(conv_n.shape[1]), row(attn.shape[1]), _layer_spec(w_o, layer)]
        if decode_in is not None:
            args += list(decode_in)
            specs += [_const_spec(a.shape) for a in decode_in]
    args += list(ffn_w)
    specs += [_const_spec(ffn_w[0].shape)] + [_layer_spec(w, layer) for w in ffn_w[1:]]
    if final_g is not None:
        args.append(final_g)
        specs.append(_const_spec(final_g.shape))
    kern = functools.partial(_ffn_kernel, mix=mix_in is not None, decode=decode_in is not None,
                             final=final_g is not None)
    return pl.pallas_call(
        kern, grid=(t // tm,), in_specs=specs, out_specs=row(d),
        out_shape=jax.ShapeDtypeStruct((t, d), F32),
        compiler_params=_params("parallel"), name=name)(*args)


def _rope_slab(x, tab):
    return (x * tab[0]
            + pltpu.roll(x, LANES - HALF_ROPE, 1) * tab[1]
            + pltpu.roll(x, HALF_ROPE, 1) * tab[2])


def _proj_common(x, tab, gmix_ref, win_ref, gq_ref, wq_ref, gkv_ref, conv_dim):
    q_lora, kv_lora = gq_ref.shape[1], gkv_ref.shape[1]
    h = _rms(x, gmix_ref[...]).astype(BF16)
    proj = jnp.dot(h, win_ref[...], preferred_element_type=F32)
    c0 = 3 * conv_dim
    b_g = proj[:, :conv_dim]
    u = proj[:, conv_dim:2 * conv_dim] * proj[:, 2 * conv_dim:c0]
    q_c = proj[:, c0:c0 + q_lora]
    kv_c = proj[:, c0 + q_lora:c0 + q_lora + kv_lora]
    kpe_slab = proj[:, c0 + q_lora + kv_lora:]
    q = jnp.dot(_rms(q_c, gq_ref[...]).astype(BF16), wq_ref[...], preferred_element_type=F32)
    q = jnp.concatenate(
        [_rope_slab(q[:, hd * LANES:(hd + 1) * LANES], tab) for hd in range(N_HEADS)], axis=-1)
    ckv = _rms(kv_c, gkv_ref[...])
    kpe_slab = _rope_slab(kpe_slab, tab)
    return b_g, u, q, ckv, kpe_slab


def _proj_prompt_kernel(x_ref, gmix_ref, win_ref, gq_ref, wq_ref, gkv_ref, wkv_ref, vone_ref,
                        convw_ref, gconv_ref, tab_ref,
                        convn_ref, q_ref, k_ref, v_ref, ckv_ref, kpe_ref, newconv_ref,
                        uext_ref, *, tiles_per_seq):
    tm = x_ref.shape[0]
    conv_dim = convw_ref.shape[1]
    k_w = N_HEADS * LANES

    @pl.when(pl.program_id(0) % tiles_per_seq == 0)
    def _():
        uext_ref[0:SUBLANES, :] = jnp.zeros((SUBLANES, conv_dim), F32)

    sub = tm // PROJ_SUBTILES
    for r0 in range(0, tm, sub):
        rows = slice(r0, r0 + sub)
        b_g, u, q, ckv, kpe_slab = _proj_common(
            x_ref[rows, :], tab_ref[:, rows, :], gmix_ref, win_ref, gq_ref, wq_ref, gkv_ref,
            conv_dim)
        uext_ref[SUBLANES + r0:SUBLANES + r0 + sub, :] = u
        u1 = uext_ref[SUBLANES - 1 + r0:SUBLANES - 1 + r0 + sub, :]
        u2 = uext_ref[SUBLANES - 2 + r0:SUBLANES - 2 + r0 + sub, :]
        conv_y = b_g * (convw_ref[0:1, :] * u2 + convw_ref[1:2, :] * u1 + convw_ref[2:3, :] * u)
        convn_ref[rows, :] = _rms(conv_y, gconv_ref[...]).astype(BF16)
        q_ref[rows, :] = q.astype(BF16)
        ckv_ref[rows, :] = ckv
        kpe_ref[rows, :] = kpe_slab[:, ROPE_LO:ROPE_LO + QK_ROPE]
        kv = jnp.dot(ckv.astype(BF16), wkv_ref[...], preferred_element_type=F32)
        k_ref[rows, :] = (kv[:, :k_w]
                          + jnp.concatenate([kpe_slab] * N_HEADS, axis=-1)).astype(BF16)
        v_ref[rows, :] = (kv[:, k_w:] + vone_ref[...]).astype(BF16)
    newconv_ref[0] = u[sub - (CONV_WIDTH - 1):, :]
    uext_ref[0:SUBLANES, :] = u[sub - SUBLANES:, :]


def _proj_decode_kernel(x_ref, gmix_ref, win_ref, gq_ref, wq_ref, gkv_ref, wuk_ref, convw_ref,
                        gconv_ref, tab_ref, s0_ref, s1_ref,
                        convn_ref, qlat_ref, q_ref, ckv_ref, kpe_ref, kpeslab_ref, u_ref):
    conv_dim = convw_ref.shape[1]
    b_g, u, q, ckv, kpe_slab = _proj_common(
        x_ref[...], tab_ref[...], gmix_ref, win_ref, gq_ref, wq_ref, gkv_ref, conv_dim)
    conv_y = b_g * (convw_ref[0:1, :] * s0_ref[...] + convw_ref[1:2, :] * s1_ref[...]
                    + convw_ref[2:3, :] * u)
    convn_ref[...] = _rms(conv_y, gconv_ref[...]).astype(BF16)
    u_ref[...] = u
    qb = q.astype(BF16)
    q_ref[...] = qb
    qlat_ref[...] = jnp.dot(qb, wuk_ref[...], preferred_element_type=F32).astype(BF16)
    ckv_ref[...] = ckv
    kpe_ref[...] = kpe_slab[:, ROPE_LO:ROPE_LO + QK_ROPE]
    kpeslab_ref[...] = kpe_slab


def _proj_prompt_call(x, w, tab, batch, seq, name):
    t, d = x.shape
    tm = min(PROJ_ROW_TILE, seq)
    assert seq % tm == 0 and tm % (PROJ_SUBTILES * SUBLANES) == 0
    tiles_per_seq = seq // tm
    conv_dim = w["conv_w"].shape[1]
    kv_lora = w["kv_a_norm"].shape[1]
    k_w = N_HEADS * LANES
    row = lambda width: pl.BlockSpec((tm, width), lambda i: (i, 0))
    consts = [w["norm_mix"], w["w_in"], w["q_a_norm"], w["w_q"], w["kv_a_norm"], w["w_kv"],
              w["v_one"], w["conv_w"], w["gn_conv"]]
    in_specs = ([row(d)] + [_const_spec(c.shape) for c in consts]
                + [pl.BlockSpec((3, tm, LANES), lambda i: (0, i % tiles_per_seq, 0))])
    out_shape = (jax.ShapeDtypeStruct((t, conv_dim), BF16),
                 jax.ShapeDtypeStruct((t, k_w), BF16),
                 jax.ShapeDtypeStruct((t, k_w), BF16),
                 jax.ShapeDtypeStruct((t, k_w), BF16),
                 jax.ShapeDtypeStruct((t, kv_lora), F32),
                 jax.ShapeDtypeStruct((t, QK_ROPE), F32),
                 jax.ShapeDtypeStruct((batch, CONV_WIDTH - 1, conv_dim), F32))
    out_specs = (row(conv_dim), row(k_w), row(k_w), row(k_w), row(kv_lora), row(QK_ROPE),
                 pl.BlockSpec((1, CONV_WIDTH - 1, conv_dim), lambda i: (i // tiles_per_seq, 0, 0)))
    return pl.pallas_call(
        functools.partial(_proj_prompt_kernel, tiles_per_seq=tiles_per_seq),
        grid=(t // tm,), in_specs=in_specs, out_specs=out_specs, out_shape=out_shape,
        scratch_shapes=[pltpu.VMEM((tm + SUBLANES, conv_dim), F32)],
        compiler_params=_params("arbitrary"), name=name)(x, *consts, tab)


def _proj_decode_call(x, w, tab, s0, s1, name):
    t, d = x.shape
    conv_dim = w["conv_w"].shape[1]
    kv_lora = w["kv_a_norm"].shape[1]
    k_w = N_HEADS * LANES
    args = [x, w["norm_mix"], w["w_in"], w["q_a_norm"], w["w_q"], w["kv_a_norm"], w["w_uk_bd"],
            w["conv_w"], w["gn_conv"], tab, s0, s1]
    full = lambda a: pl.BlockSpec(a.shape, lambda i: (0,) * a.ndim)
    out_shape = (jax.ShapeDtypeStruct((t, conv_dim), BF16),
                 jax.ShapeDtypeStruct((t, k_w), BF16),
                 jax.ShapeDtypeStruct((t, k_w), BF16),
                 jax.ShapeDtypeStruct((t, kv_lora), F32),
                 jax.ShapeDtypeStruct((t, QK_ROPE), F32),
                 jax.ShapeDtypeStruct((t, LANES), F32),
                 jax.ShapeDtypeStruct((t, conv_dim), F32))
    return pl.pallas_call(
        _proj_decode_kernel, grid=(1,), in_specs=[full(a) for a in args],
        out_specs=tuple(pl.BlockSpec(s.shape, lambda i: (0, 0)) for s in out_shape),
        out_shape=out_shape, compiler_params=_params("arbitrary"), name=name)(*args)


def _attn_kernel(q_ref, k_ref, v_ref, g_ref, o_ref, acc_ref, m_ref, *, exp_scale):
    tq = q_ref.shape[1]
    tk = ATTN_K_TILE if tq % ATTN_K_TILE == 0 else tq
    blocks_per_tile = tq // tk
    i = pl.program_id(1)
    acc_ref[...] = jnp.zeros(acc_ref.shape, F32)
    m_ref[...] = jnp.full(m_ref.shape, -jnp.inf, F32)

    def block(j, r0, diag):
        rows = slice(r0, tq)
        start = pl.multiple_of(j * tk, tk)
        if diag:
            causal = (lax.broadcasted_iota(jnp.int32, (tq - r0, tk), 1)
                      <= lax.broadcasted_iota(jnp.int32, (tq - r0, tk), 0))
        for hd in range(N_HEADS):
            lanes = slice(hd * LANES, (hd + 1) * LANES)
            s = lax.dot_general(q_ref[0, rows, lanes], k_ref[0, pl.ds(start, tk), lanes],
                                (((1,), (1,)), ((), ())), preferred_element_type=F32)
            if diag:
                s = jnp.where(causal, s, -jnp.inf)
            m_old = m_ref[hd, rows, :]
            m_new = jnp.maximum(m_old, jnp.max(s, axis=-1, keepdims=True))
            p = jnp.exp2((s - jnp.concatenate([m_new] * (tk // LANES), axis=-1)) * exp_scale)
            alpha = jnp.exp2((m_old - m_new) * exp_scale)
            acc_ref[hd, rows, :] = alpha * acc_ref[hd, rows, :] + jnp.dot(
                p.astype(BF16), v_ref[0, pl.ds(start, tk), lanes], preferred_element_type=F32)
            m_ref[hd, rows, :] = m_new

    def body(j, c):
        block(j, 0, False)
        return c
    lax.fori_loop(0, i * blocks_per_tile, body, 0)
    one = jnp.minimum(i + 1, 1)
    for d in range(blocks_per_tile):
        def diag_body(_, c, d=d):
            block(i * blocks_per_tile + d, d * tk, True)
            return c
        lax.fori_loop(0, one, diag_body, 0)

    lane = lax.broadcasted_iota(jnp.int32, (tq, LANES), 1)
    outs = []
    for pr in range(N_HEADS // 2):
        a, b = acc_ref[2 * pr], acc_ref[2 * pr + 1]
        oa = a * (1.0 / a[:, V_HEAD:V_HEAD + 1])
        ob = b * (1.0 / b[:, 0:1])
        outs.append(jnp.where(lane < V_HEAD, oa, ob))
    o_ref[0] = _rms(jnp.concatenate(outs, axis=-1), g_ref[...]).astype(BF16)


def _attn_call(q, k, v, gn_attn, name):
    b, s, w = q.shape
    tq = min(ATTN_Q_TILE, s)
    assert s % tq == 0 and 2 * V_HEAD == LANES and N_HEADS % 2 == 0
    exp_scale = float((QK_NOPE + QK_ROPE) ** -0.5 * LOG2_E)
    return pl.pallas_call(
        functools.partial(_attn_kernel, exp_scale=exp_scale),
        grid=(b, s // tq),
        in_specs=[pl.BlockSpec((1, tq, w), lambda bi, i: (bi, i, 0)),
                  pl.BlockSpec((1, s, w), lambda bi, i: (bi, 0, 0)),
                  pl.BlockSpec((1, s, w), lambda bi, i: (bi, 0, 0)),
                  _const_spec(gn_attn.shape)],
        out_specs=pl.BlockSpec((1, tq, N_HEADS * V_HEAD), lambda bi, i: (bi, i, 0)),
        out_shape=jax.ShapeDtypeStruct((b, s, N_HEADS * V_HEAD), BF16),
        scratch_shapes=[pltpu.VMEM((N_HEADS, tq, LANES), F32),
                        pltpu.VMEM((N_HEADS, tq, LANES), F32)],
        compiler_params=_params("parallel", "arbitrary"), name=name)(q, k, v, gn_attn)


def _decode_attn_kernel(pt_ref, qlat_ref, q_ref, ckvn_ref, kpen_ref, cckv_hbm, ckpe_hbm,
                        o_ref, ckv_buf, kpe_buf, sem, *, layer, scale):
    g = pl.program_id(0)
    ng = pl.num_programs(0)
    _, rows, n_pages, page_size, _ = ckv_buf.shape
    slot = g % 2

    def copies(gg, sl, r, p):
        page = pt_ref[gg * rows + r, p]
        lane0 = pl.multiple_of(p * page_size, page_size)
        return (pltpu.make_async_copy(cckv_hbm.at[layer, page], ckv_buf.at[sl, r, p], sem.at[0, sl]),
                pltpu.make_async_copy(ckpe_hbm.at[layer, page],
                                      kpe_buf.at[sl, r, :, pl.ds(lane0, page_size)], sem.at[1, sl]))

    def start_fetch(gg, sl):
        def body(p, c):
            for r in range(rows):
                for cp in copies(gg, sl, r, p):
                    cp.start()
            return c
        lax.fori_loop(0, n_pages, body, 0, unroll=DMA_ISSUE_UNROLL)

    @pl.when(g == 0)
    def _():
        start_fetch(0, 0)

    @pl.when(g + 1 < ng)
    def _():
        start_fetch(g + 1, 1 - slot)

    for r in range(rows):
        pltpu.make_async_copy(cckv_hbm.at[layer, pl.ds(0, n_pages)], ckv_buf.at[slot, r],
                              sem.at[0, slot]).wait()
        pltpu.make_async_copy(kpe_buf.at[1 - slot, r], kpe_buf.at[slot, r], sem.at[1, slot]).wait()

    for r in range(rows):
        q_lat = qlat_ref[r]
        q_slab = q_ref[r]
        q_pe = q_slab[:, ROPE_LO:ROPE_LO + QK_ROPE]
        ckv = ckv_buf[slot, r]
        ckv = ckv.reshape(n_pages * page_size, ckv.shape[2]).astype(BF16)
        s_rope = jnp.dot(q_pe, kpe_buf[slot, r].astype(BF16), preferred_element_type=F32)
        s = (lax.dot_general(q_lat, ckv, (((1,), (1,)), ((), ())), preferred_element_type=F32)
             + s_rope) * scale
        ckv_new = ckvn_ref[r]
        s_new = (jnp.sum(q_lat.astype(F32) * ckv_new, axis=-1, keepdims=True)
                 + jnp.sum(q_slab.astype(F32) * kpen_ref[r], axis=-1, keepdims=True)) * scale
        m = jnp.maximum(jnp.max(s, axis=-1, keepdims=True), s_new)
        p = jnp.exp(s - m)
        p_new = jnp.exp(s_new - m)
        l = jnp.sum(p, axis=-1, keepdims=True) + p_new
        o = jnp.dot(p.astype(BF16), ckv, preferred_element_type=F32) + p_new * ckv_new
        o_ref[r] = o / l


def _decode_attn_call(page_table, qlat, q, ckv_new, kpe_new_slab, cache_ckv, cache_kpe_t, layer,
                      name):
    nb, n_pages = page_table.shape
    page, kv_lora = cache_ckv.shape[2], cache_ckv.shape[3]
    assert kv_lora == LANES and cache_kpe_t.shape[2:] == (QK_ROPE, page)
    rows = DECODE_ROWS if nb % DECODE_ROWS == 0 else 1
    heads3 = lambda a: a.reshape(nb, N_HEADS, LANES)
    one3 = lambda a: a.reshape(nb, 1, LANES)
    per_b = lambda sub: pl.BlockSpec((rows, sub, LANES), lambda g, pt: (g, 0, 0))
    grid_spec = pltpu.PrefetchScalarGridSpec(
        num_scalar_prefetch=1, grid=(nb // rows,),
        in_specs=[per_b(N_HEADS), per_b(N_HEADS), per_b(1), per_b(1),
                  pl.BlockSpec(memory_space=pl.ANY), pl.BlockSpec(memory_space=pl.ANY)],
        out_specs=per_b(N_HEADS),
        scratch_shapes=[pltpu.VMEM((2, rows, n_pages, page, kv_lora), F32),
                        pltpu.VMEM((2, rows, QK_ROPE, n_pages * page), F32),
                        pltpu.SemaphoreType.DMA((2, 2))])
    scale = float((QK_NOPE + QK_ROPE) ** -0.5)
    out = pl.pallas_call(
        functools.partial(_decode_attn_kernel, layer=layer, scale=scale),
        grid_spec=grid_spec,
        out_shape=jax.ShapeDtypeStruct((nb, N_HEADS, LANES), F32),
        compiler_params=_params("arbitrary"), name=name)(
            page_table, heads3(qlat), heads3(q), one3(ckv_new), one3(kpe_new_slab),
            cache_ckv, cache_kpe_t)
    return out.reshape(nb, N_HEADS * LANES)


def _rope_tables(pos):
    inv_freq = 1.0 / (ROPE_THETA ** (jnp.arange(HALF_ROPE, dtype=F32) / HALF_ROPE))
    ang = pos.astype(F32)[:, None] * inv_freq[None, :]
    cos, sin = jnp.cos(ang), jnp.sin(ang)
    n = pos.shape[0]
    one = jnp.ones((n, ROPE_LO), F32)
    zero = lambda w: jnp.zeros((n, w), F32)
    tail = LANES - ROPE_LO - QK_ROPE
    c = jnp.concatenate([one, cos, cos, zero(tail)], axis=-1)
    s1 = jnp.concatenate([zero(ROPE_LO), -sin, zero(HALF_ROPE + tail)], axis=-1)
    s2 = jnp.concatenate([zero(ROPE_LO + HALF_ROPE), sin, zero(tail)], axis=-1)
    return jnp.stack([c, s1, s2])


def _layer_weights(l, norm_mix, w_in, conv_w, q_a_norm, w_q_b, kv_a_norm, w_kv_b, gn_conv,
                   gn_attn):
    d = w_in.shape[1]
    conv_dim, q_lora, kv_lora = conv_w.shape[2], q_a_norm.shape[1], kv_a_norm.shape[1]
    c1 = 3 * conv_dim + q_lora + kv_lora
    tail = LANES - ROPE_LO - QK_ROPE
    win = w_in[l]
    win_ext = jnp.concatenate(
        [win[:, :c1], jnp.zeros((d, ROPE_LO), F32), win[:, c1:], jnp.zeros((d, tail), F32)],
        axis=-1).astype(BF16)
    wq = w_q_b[l].reshape(q_lora, N_HEADS, QK_NOPE + QK_ROPE)
    wq_slab = jnp.concatenate([wq, jnp.zeros((q_lora, N_HEADS, tail), F32)], axis=-1)
    wq_slab = wq_slab.reshape(q_lora, N_HEADS * LANES).astype(BF16)
    wkv = w_kv_b[l].reshape(kv_lora, N_HEADS, QK_NOPE + V_HEAD)
    w_uk, w_uv = wkv[..., :QK_NOPE], wkv[..., QK_NOPE:]
    wk_slab = jnp.concatenate([w_uk, jnp.zeros((kv_lora, N_HEADS, LANES - QK_NOPE), F32)], axis=-1)
    pad = jnp.zeros((kv_lora, N_HEADS // 2, LANES - V_HEAD), F32)
    wv_slab = jnp.stack([jnp.concatenate([w_uv[:, 0::2], pad], axis=-1),
                         jnp.concatenate([pad, w_uv[:, 1::2]], axis=-1)], axis=2)
    wkv_ext = jnp.concatenate([wk_slab.reshape(kv_lora, N_HEADS * LANES),
                               wv_slab.reshape(kv_lora, N_HEADS * LANES)], axis=-1).astype(BF16)
    spare = jnp.zeros((LANES - V_HEAD,), F32).at[0].set(1.0)
    v_one = jnp.tile(jnp.concatenate([jnp.zeros((V_HEAD,), F32), spare, spare,
                                      jnp.zeros((V_HEAD,), F32)]), N_HEADS // 2)[None]
    eye = jnp.eye(N_HEADS, dtype=F32)
    uk_rows = jnp.concatenate(
        [jnp.transpose(w_uk, (1, 2, 0)), jnp.zeros((N_HEADS, LANES - QK_NOPE, kv_lora), F32)],
        axis=1)
    w_uk_bd = (uk_rows[:, :, None, :] * eye[:, None, :, None]).reshape(
        N_HEADS * LANES, N_HEADS * kv_lora).astype(BF16)
    w_uv_bd = (jnp.transpose(w_uv, (1, 0, 2))[:, :, None, :] * eye[:, None, :, None]).reshape(
        N_HEADS * kv_lora, N_HEADS * V_HEAD).astype(BF16)
    return dict(norm_mix=norm_mix[l][None], w_in=win_ext, q_a_norm=q_a_norm[l][None], w_q=wq_slab,
                kv_a_norm=kv_a_norm[l][None], w_kv=wkv_ext, v_one=v_one, w_uk_bd=w_uk_bd,
                w_uv_bd=w_uv_bd, conv_w=conv_w[l], gn_conv=gn_conv[l][None],
                gn_attn=gn_attn[l][None])


def kernel(x_prompt, x_sample, state_conv, cache_ckv, cache_kpe, page_table, norm_ffn1, w_ffn1_gate, w_ffn1_up, w_ffn1_down, norm_mix, w_in, conv_w, q_a_norm, w_q_b, kv_a_norm, w_kv_b, gn_conv, gn_attn, w_o, norm_ffn2, w_ffn2_gate, w_ffn2_up, w_ffn2_down, final_norm):
    batch, seq, d = x_prompt.shape
    dec_batch, dec_seq, _ = x_sample.shape
    assert dec_seq == 1
    depth = norm_mix.shape[0]
    past_len = page_table.shape[1] * cache_ckv.shape[2]
    kv_lora = kv_a_norm.shape[1]

    tab_p = _rope_tables(jnp.arange(seq, dtype=jnp.int32))
    tab_s = jnp.broadcast_to(_rope_tables(past_len + jnp.arange(dec_seq, dtype=jnp.int32)),
                             (3, dec_batch, LANES))
    cache_kpe_t = jnp.swapaxes(cache_kpe, 2, 3)
    yp = x_prompt.reshape(batch * seq, d)
    ys = x_sample.reshape(dec_batch, d)
    fin = final_norm[None]
    ffn1_w = [_cast_call(w, name=f"cast_ffn1_{n}") for n, w in
              enumerate((w_ffn1_gate, w_ffn1_up, w_ffn1_down))]
    ffn2_w = [_cast_call(w, name=f"cast_ffn2_{n}") for n, w in
              enumerate((w_ffn2_gate, w_ffn2_up, w_ffn2_down))]
    w_o_bf = _cast_call(w_o, name="cast_w_o")
    conv_p, ckv_p, kpe_p, conv_s, ckv_s, kpe_s = [], [], [], [], [], []
    for l in range(depth):
        w = _layer_weights(l, norm_mix, w_in, conv_w, q_a_norm, w_q_b, kv_a_norm, w_kv_b, gn_conv,
                           gn_attn)
        ffn1 = (norm_ffn1[l][None], *ffn1_w)
        ffn2 = (norm_ffn2[l][None], *ffn2_w)
        last = fin if l == depth - 1 else None

        yp = _ffn_call(yp, l, ffn1, name=f"ffn1_p{l}")
        conv_n, q, k, v, ckv, kpe, new_conv = _proj_prompt_call(yp, w, tab_p, batch, seq,
                                                                name=f"proj_p{l}")
        attn = _attn_call(q.reshape(batch, seq, -1), k.reshape(batch, seq, -1),
                          v.reshape(batch, seq, -1), w["gn_attn"], name=f"attn_p{l}")
        yp = _ffn_call(yp, l, ffn2, mix_in=(conv_n, attn.reshape(batch * seq, -1), w_o_bf),
                       final_g=last, name=f"mix_ffn2_p{l}")
        conv_p.append(new_conv)
        ckv_p.append(ckv.reshape(batch, seq, kv_lora))
        kpe_p.append(kpe.reshape(batch, seq, QK_ROPE))

        ys = _ffn_call(ys, l, ffn1, name=f"ffn1_s{l}")
        s0, s1 = state_conv[l][:, 0, :], state_conv[l][:, 1, :]
        conv_n, qlat, q, ckv, kpe, kpe_slab, u = _proj_decode_call(ys, w, tab_s, s0, s1,
                                                                   name=f"proj_s{l}")
        o_lat = _decode_attn_call(page_table, qlat, q, ckv, kpe_slab, cache_ckv, cache_kpe_t, l,
                                  name=f"attn_s{l}")
        ys = _ffn_call(ys, l, ffn2, mix_in=(conv_n, o_lat, w_o_bf),
                       decode_in=(w["w_uv_bd"], w["gn_attn"]), final_g=last,
                       name=f"mix_ffn2_s{l}")
        conv_s.append(jnp.stack([s1, u], axis=1))
        ckv_s.append(ckv.reshape(dec_batch, dec_seq, kv_lora))
        kpe_s.append(kpe.reshape(dec_batch, dec_seq, QK_ROPE))

    return (yp.reshape(batch, seq, d), ys.reshape(dec_batch, dec_seq, d),
            jnp.stack(conv_p), jnp.stack(ckv_p), jnp.stack(kpe_p),
            jnp.stack(conv_s), jnp.stack(ckv_s), jnp.stack(kpe_s))
```
